```python
import math
import jax, jax.numpy as jnp
from jax import lax
import numpy as np

D_MODEL = 1024
BATCH = 8
SEQ = 4096
DEPTH = 1

PLE_DIM = 256
ROPE_THETA = 10000.0
RMS_EPS = 1e-6
Q_BLOCK = 128

D_MIX = D_MODEL
DIFF_WIDTH = D_MIX // 2
DIFF_HEADS = 4
DIFF_HD = DIFF_WIDTH // (2 * DIFF_HEADS)
MLA_WIDTH = D_MIX - DIFF_WIDTH
MLA_HEADS = 8
MLA_NOPE = 64
MLA_ROPE = 32
MLA_V = MLA_WIDTH // MLA_HEADS
MLA_Q_LORA = 384
MLA_KV_LORA = 128
SPLIT_SIZES = (DIFF_WIDTH, DIFF_WIDTH, DIFF_WIDTH, DIFF_WIDTH,
               MLA_Q_LORA, MLA_KV_LORA, MLA_ROPE, MLA_WIDTH)
D_IN = sum(SPLIT_SIZES)

kernel_name = "hymba_diffattn_mla_parallel_heads"


def rms_norm(x, g):
    xf = x.astype(jnp.float32)
    y = xf * lax.rsqrt(jnp.mean(xf * xf, axis=-1, keepdims=True) + RMS_EPS)
    return (y * g.astype(jnp.float32)).astype(x.dtype)


def rope(x, pos):
    d = x.shape[-1]
    inv = ROPE_THETA ** (-jnp.arange(0, d, 2, dtype=jnp.float32) / d)
    ang = pos.astype(jnp.float32)[..., None] * inv
    cos = jnp.cos(ang)[:, :, None, :]
    sin = jnp.sin(ang)[:, :, None, :]
    xf = x.astype(jnp.float32)
    x1, x2 = xf[..., : d // 2], xf[..., d // 2:]
    return jnp.concatenate([x1 * cos - x2 * sin, x2 * cos + x1 * sin], axis=-1).astype(x.dtype)


def to_blocks(t):
    b, s = t.shape[:2]
    return t.reshape(b, s // Q_BLOCK, Q_BLOCK, *t.shape[2:]).swapaxes(0, 1)


def from_blocks(t):
    nb, b, blk = t.shape[:3]
    return t.swapaxes(0, 1).reshape(b, nb * blk, *t.shape[3:])


def causal_probs(scores, start, seq):
    q_idx = start + jnp.arange(Q_BLOCK)
    k_idx = jnp.arange(seq)
    mask = k_idx[None, :] <= q_idx[:, None]
    return jax.nn.softmax(jnp.where(mask, scores, -jnp.inf), axis=-1)


def diff_attention(q, k, v, lam, pos):
    b, s, h, _, d = q.shape
    q = rope(q.reshape(b, s, 2 * h, d), pos).reshape(b, s, h, 2, d)
    k = rope(k.reshape(b, s, 2 * h, d), pos).reshape(b, s, h, 2, d)
    scale = d ** -0.5

    def block(args):
        qb, start = args
        sc = jnp.einsum('bqhmd,bkhmd->mbhqk', qb, k).astype(jnp.float32) * scale
        pr = causal_probs(sc, start, s)
        a = pr[0] - lam * pr[1]
        return jnp.einsum('bhqk,bkhe->bqhe', a.astype(v.dtype), v)

    starts = jnp.arange(s // Q_BLOCK) * Q_BLOCK
    return from_blocks(lax.map(block, (to_blocks(q), starts)))


def mla_attention(q_nope, q_rope, k_nope, k_rope, v):
    s = q_nope.shape[1]
    scale = (MLA_NOPE + MLA_ROPE) ** -0.5

    def block(args):
        qn, qr, start = args
        sc = (jnp.einsum('bqhd,bkhd->bhqk', qn, k_nope)
              + jnp.einsum('bqhr,bkr->bhqk', qr, k_rope)).astype(jnp.float32) * scale
        pr = causal_probs(sc, start, s)
        return jnp.einsum('bhqk,bkhd->bqhd', pr.astype(v.dtype), v)

    starts = jnp.arange(s // Q_BLOCK) * Q_BLOCK
    return from_blocks(lax.map(block, (to_blocks(q_nope), to_blocks(q_rope), starts)))


def setup_inputs(seed: int = 0) -> dict:
    key = jax.random.key(seed)
    ks = jax.random.split(key, 16)
    nrm = jax.random.normal
    f32 = jnp.float32
    x = nrm(ks[0], (BATCH, SEQ, D_MODEL), f32)
    p = nrm(ks[1], (DEPTH, BATCH, SEQ, PLE_DIM), f32)
    offset = jax.random.randint(ks[2], (BATCH, 1), 0, 1024, dtype=jnp.int32)
    positions = (offset + jnp.arange(SEQ, dtype=jnp.int32)[None, :]).astype(jnp.int32)
    norm_g = 1.0 + 0.02 * nrm(ks[3], (DEPTH, D_MODEL), f32)
    w_in = nrm(ks[4], (DEPTH, D_MODEL, D_IN), f32) * D_MODEL ** -0.5
    diff_lambda = 0.1 * nrm(ks[5], (DEPTH, 4, DIFF_HD), f32)
    diff_subln_g = 1.0 + 0.02 * nrm(ks[6], (DEPTH, 2 * DIFF_HD), f32)
    mla_q_norm_g = 1.0 + 0.02 * nrm(ks[7], (DEPTH, MLA_Q_LORA), f32)
    w_uq = nrm(ks[8], (DEPTH, MLA_Q_LORA, MLA_HEADS * (MLA_NOPE + MLA_ROPE)), f32) * MLA_Q_LORA ** -0.5
    mla_kv_norm_g = 1.0 + 0.02 * nrm(ks[9], (DEPTH, MLA_KV_LORA), f32)
    w_ukv = nrm(ks[10], (DEPTH, MLA_KV_LORA, MLA_HEADS * (MLA_NOPE + MLA_V)), f32) * MLA_KV_LORA ** -0.5
    w_out = nrm(ks[11], (DEPTH, D_MIX, D_MODEL), f32) * D_MIX ** -0.5
    w_ple = nrm(ks[12], (DEPTH, PLE_DIM, D_MODEL), f32) * PLE_DIM ** -0.5
    w_ple_gate = nrm(ks[13], (DEPTH, D_MODEL, D_MODEL), f32) * D_MODEL ** -0.5
    final_norm_g = 1.0 + 0.02 * nrm(ks[14], (D_MODEL,), f32)
    return {"x": x, "p": p, "positions": positions, "norm_g": norm_g, "w_in": w_in,
            "diff_lambda": diff_lambda, "diff_subln_g": diff_subln_g,
            "mla_q_norm_g": mla_q_norm_g, "w_uq": w_uq, "mla_kv_norm_g": mla_kv_norm_g,
            "w_ukv": w_ukv, "w_out": w_out, "w_ple": w_ple, "w_ple_gate": w_ple_gate,
            "final_norm_g": final_norm_g}


def reference(x, p, positions, norm_g, w_in, diff_lambda, diff_subln_g, mla_q_norm_g, w_uq,
              mla_kv_norm_g, w_ukv, w_out, w_ple, w_ple_gate, final_norm_g):
    b, s, _ = x.shape
    offsets = [int(o) for o in np.cumsum(SPLIT_SIZES)[:-1]]
    h = x
    for i in range(DEPTH):
        n = rms_norm(h, norm_g[i])
        proj = n @ w_in[i]
        dq, dk, dv, dgate, cq, ckv, kr, mgate = jnp.split(proj, offsets, axis=-1)

        lq1, lk1, lq2, lk2 = diff_lambda[i].astype(jnp.float32)
        lam_init = 0.8 - 0.6 * math.exp(-0.3 * i)
        lam = jnp.exp(jnp.sum(lq1 * lk1)) - jnp.exp(jnp.sum(lq2 * lk2)) + lam_init
        od = diff_attention(dq.reshape(b, s, DIFF_HEADS, 2, DIFF_HD),
                            dk.reshape(b, s, DIFF_HEADS, 2, DIFF_HD),
                            dv.reshape(b, s, DIFF_HEADS, 2 * DIFF_HD), lam, positions)
        od = rms_norm(od, diff_subln_g[i]) * (1.0 - lam_init)
        od = od.reshape(b, s, DIFF_WIDTH) * jax.nn.silu(dgate)

        q = (rms_norm(cq, mla_q_norm_g[i]) @ w_uq[i]).reshape(b, s, MLA_HEADS, MLA_NOPE + MLA_ROPE)
        q_nope = q[..., :MLA_NOPE]
        q_rope = rope(q[..., MLA_NOPE:], positions)
        kv = (rms_norm(ckv, mla_kv_norm_g[i]) @ w_ukv[i]).reshape(b, s, MLA_HEADS, MLA_NOPE + MLA_V)
        k_nope, v = kv[..., :MLA_NOPE], kv[..., MLA_NOPE:]
        k_rope = rope(kr[:, :, None, :], positions)[:, :, 0, :]
        om = mla_attention(q_nope, q_rope, k_nope, k_rope, v).reshape(b, s, MLA_WIDTH)
        om = om * jax.nn.silu(mgate)

        h = h + jnp.concatenate([od, om], axis=-1) @ w_out[i]

        h = h + (p[i] @ w_ple[i]) * jax.nn.sigmoid(h @ w_ple_gate[i])
    return rms_norm(h, final_norm_g)
```

```python
import math
from functools import partial

import jax
import jax.numpy as jnp
from jax import lax
from jax.experimental import pallas as pl
from jax.experimental.pallas import tpu as pltpu

D_MODEL = 1024
PLE_DIM = 256
ROPE_THETA = 10000.0
RMS_EPS = 1e-6

DIFF_WIDTH = 512
DIFF_HEADS = 4
DIFF_HD = 64
MLA_WIDTH = 512
MLA_HEADS = 8
MLA_NOPE = 64
MLA_ROPE = 32
MLA_V = 64
MLA_Q_LORA = 384
MLA_KV_LORA = 128

LANES = 128
LOG2E = math.log2(math.e)
VMEM_LIMIT = 48 * 1024 * 1024

O_DQ, O_DK, O_DV, O_DG = 0, 512, 1024, 1536
O_CQ = 2048
O_CKV = O_CQ + MLA_Q_LORA
O_MG = O_CKV + MLA_KV_LORA
O_KR = O_MG + MLA_WIDTH
D_IN_P = O_KR + LANES

PROJ_BM = 512
ATT_BLK = 512


def _rms(x, g):
    return x * lax.rsqrt(jnp.mean(x * x, axis=-1, keepdims=True) + RMS_EPS) * g


def _silu(x):
    return x / (1.0 + jnp.exp(-x))


def _rope128(x, c, sp, sm, half):
    return (x * c + pltpu.roll(x, half, axis=1) * sp
            + pltpu.roll(x, LANES - half, axis=1) * sm)


def _proj_kernel(x_ref, pos_ref, tab_ref, ng_ref, w_ref, qg_ref, wuq_ref, kvg_ref,
                 wk_ref, wv_ref,
                 dq_ref, dk_ref, dv_ref, dg_ref, mq_ref, mk_ref, mv_ref, mg_ref):
    f32 = jnp.float32
    bf16 = jnp.bfloat16
    x = x_ref[...]
    nb = _rms(x, ng_ref[...]).astype(bf16)

    pos = pos_ref[...]
    ang_d = pos * tab_ref[0:1, :]
    cd = jnp.cos(ang_d)
    sd = jnp.sin(ang_d)
    spd = sd * tab_ref[1:2, :]
    smd = sd * tab_ref[2:3, :]
    ang_m = pos * tab_ref[3:4, :]
    cm = jnp.cos(ang_m)
    s_m = jnp.sin(ang_m)
    spm = s_m * tab_ref[4:5, :]
    smm = s_m * tab_ref[5:6, :]

    def proj(off, width):
        return jnp.dot(nb, w_ref[:, off:off + width], preferred_element_type=f32)

    q_scale = DIFF_HD ** -0.5 * LOG2E
    dq = proj(O_DQ, DIFF_WIDTH)
    dk = proj(O_DK, DIFF_WIDTH)
    for c in range(DIFF_WIDTH // LANES):
        sl = slice(c * LANES, (c + 1) * LANES)
        dq_ref[:, sl] = (_rope128(dq[:, sl], cd, spd, smd, DIFF_HD // 2) * q_scale).astype(bf16)
        dk_ref[:, sl] = _rope128(dk[:, sl], cd, spd, smd, DIFF_HD // 2).astype(bf16)
    dv_ref[...] = proj(O_DV, DIFF_WIDTH).astype(bf16)
    dg_ref[...] = _silu(proj(O_DG, DIFF_WIDTH)).astype(bf16)
    mg_ref[...] = _silu(proj(O_MG, MLA_WIDTH)).astype(bf16)

    cq = proj(O_CQ, MLA_Q_LORA)
    cqn = _rms(cq, qg_ref[...]).astype(bf16)
    mq = jnp.dot(cqn, wuq_ref[...], preferred_element_type=f32)
    m_scale = (MLA_NOPE + MLA_ROPE) ** -0.5 * LOG2E

    ckv = proj(O_CKV, MLA_KV_LORA)
    ckvn = _rms(ckv, kvg_ref[...]).astype(bf16)
    kn = jnp.dot(ckvn, wk_ref[...], preferred_element_type=f32)
    mv_ref[...] = jnp.dot(ckvn, wv_ref[...], preferred_element_type=f32).astype(bf16)
    kr = _rope128(proj(O_KR, LANES), cm, spm, smm, MLA_ROPE // 2)
    for h in range(MLA_HEADS):
        sl = slice(h * LANES, (h + 1) * LANES)
        mq_ref[:, sl] = (_rope128(mq[:, sl], cm, spm, smm, MLA_ROPE // 2) * m_scale).astype(bf16)
        mk_ref[:, sl] = (kn[:, sl] + kr).astype(bf16)


def _proj_call(x2, posb, tab, ng, w_in_p, qg, wuq_p, kvg, wk_p, wv_p):
    t = x2.shape[0]
    bm = PROJ_BM
    bf16 = jnp.bfloat16
    row = lambda w: pl.BlockSpec((bm, w), lambda i: (i, 0))
    full = lambda a: pl.BlockSpec(a.shape, lambda i: (0, 0))
    outs = [(DIFF_WIDTH,)] * 4 + [(MLA_HEADS * LANES,)] * 2 + [(MLA_WIDTH,)] * 2
    return pl.pallas_call(
        _proj_kernel,
        grid=(t // bm,),
        in_specs=[row(D_MODEL), row(LANES), full(tab), full(ng), full(w_in_p), full(qg),
                  full(wuq_p), full(kvg), full(wk_p), full(wv_p)],
        out_specs=[row(w) for (w,) in outs],
        out_shape=[jax.ShapeDtypeStruct((t, w), bf16) for (w,) in outs],
        compiler_params=pltpu.CompilerParams(
            dimension_semantics=("arbitrary",), vmem_limit_bytes=VMEM_LIMIT),
        name="proj",
    )(x2, posb, tab, ng, w_in_p, qg, wuq_p, kvg, wk_p, wv_p)


def _attn_two_maps(get_q, get_k, v_ref, m_ref, l_ref, acc_ref):
    f32 = jnp.float32
    blk = ATT_BLK
    qi = pl.program_id(2)
    m_ref[...] = jnp.full(m_ref.shape, -jnp.inf, f32)
    l_ref[...] = jnp.zeros(l_ref.shape, f32)
    acc_ref[...] = jnp.zeros(acc_ref.shape, f32)

    def step(start, masked):
        v = v_ref[0, pl.ds(start, blk), :]
        for i in range(2):
            s = lax.dot_general(get_q(i), get_k(i, start), (((1,), (1,)), ((), ())),
                                preferred_element_type=f32)
            if masked:
                row = lax.broadcasted_iota(jnp.int32, s.shape, 0)
                col = lax.broadcasted_iota(jnp.int32, s.shape, 1)
                s = jnp.where(col <= row, s, -jnp.inf)
            m_prev = m_ref[i]
            m_new = jnp.maximum(m_prev, jnp.max(s, axis=1, keepdims=True))
            alpha = jnp.exp2(m_prev - m_new)
            p = jnp.exp2(s - m_new)
            l_ref[i] = alpha * l_ref[i] + jnp.sum(p, axis=1, keepdims=True)
            acc_ref[i] = alpha * acc_ref[i] + jnp.dot(
                p.astype(jnp.bfloat16), v, preferred_element_type=f32)
            m_ref[i] = m_new

    def body(j, carry):
        step(pl.multiple_of(j * blk, blk), False)
        return carry

    lax.fori_loop(0, qi, body, 0)
    step(pl.multiple_of(qi * blk, blk), True)


def _diff_kernel(lam_ref, q_ref, k_ref, v_ref, g_ref, sg_ref, o_ref,
                 qz_ref, m_ref, l_ref, acc_ref):
    q = q_ref[0]
    lane = lax.broadcasted_iota(jnp.int32, q.shape, 1)
    zero = jnp.zeros_like(q)
    qz_ref[0] = jnp.where(lane < DIFF_HD, q, zero)
    qz_ref[1] = jnp.where(lane >= DIFF_HD, q, zero)

    _attn_two_maps(lambda i: qz_ref[i],
                   lambda i, start: k_ref[0, pl.ds(start, ATT_BLK), :],
                   v_ref, m_ref, l_ref, acc_ref)

    lam_rows = lam_ref[...]
    lam_init = 0.8 - 0.6 * math.exp(-0.3 * 0)
    lam = (jnp.exp(jnp.sum(lam_rows[0:1] * lam_rows[1:2], axis=1, keepdims=True))
           - jnp.exp(jnp.sum(lam_rows[2:3] * lam_rows[3:4], axis=1, keepdims=True))
           + lam_init)
    o = acc_ref[0] / l_ref[0] - lam * (acc_ref[1] / l_ref[1])
    o = _rms(o, sg_ref[...]) * (1.0 - lam_init)
    o_ref[0] = (o * g_ref[0].astype(jnp.float32)).astype(o_ref.dtype)


def _mla_kernel(q_ref, k_ref, v_ref, g_ref, o_ref, m_ref, l_ref, acc_ref):
    _attn_two_maps(lambda i: q_ref[0, :, i * LANES:(i + 1) * LANES],
                   lambda i, start: k_ref[0, pl.ds(start, ATT_BLK), i * LANES:(i + 1) * LANES],
                   v_ref, m_ref, l_ref, acc_ref)
    o0 = acc_ref[0] / l_ref[0]
    o1 = acc_ref[1] / l_ref[1]
    lane = lax.broadcasted_iota(jnp.int32, o0.shape, 1)
    o = jnp.where(lane < MLA_V, o0, o1)
    o_ref[0] = (o * g_ref[0].astype(jnp.float32)).astype(o_ref.dtype)


def _attn_scratch():
    f32 = jnp.float32
    return [pltpu.VMEM((2, ATT_BLK, 1), f32), pltpu.VMEM((2, ATT_BLK, 1), f32),
            pltpu.VMEM((2, ATT_BLK, LANES), f32)]


def _diff_call(lam_rows, dq, dk, dv, dg, subg):
    b, s, _ = dq.shape
    blk = ATT_BLK
    qspec = pl.BlockSpec((1, blk, LANES), lambda bi, h, qi: (bi, qi, h))
    kvspec = pl.BlockSpec((1, s, LANES), lambda bi, h, qi: (bi, 0, h))
    return pl.pallas_call(
        _diff_kernel,
        grid=(b, DIFF_HEADS, s // blk),
        in_specs=[pl.BlockSpec(lam_rows.shape, lambda bi, h, qi: (0, 0)),
                  qspec, kvspec, kvspec, qspec,
                  pl.BlockSpec(subg.shape, lambda bi, h, qi: (0, 0))],
        out_specs=qspec,
        out_shape=jax.ShapeDtypeStruct((b, s, DIFF_WIDTH), jnp.bfloat16),
        scratch_shapes=[pltpu.VMEM((2, blk, LANES), jnp.bfloat16)] + _attn_scratch(),
        compiler_params=pltpu.CompilerParams(
            dimension_semantics=("arbitrary",) * 3, vmem_limit_bytes=VMEM_LIMIT),
        name="diffattn",
    )(lam_rows, dq, dk, dv, dg, subg)


def _mla_call(mq, mk, mv, mg):
    b, s, _ = mq.shape
    blk = ATT_BLK
    pairs = MLA_HEADS // 2
    return pl.pallas_call(
        _mla_kernel,
        grid=(b, pairs, s // blk),
        in_specs=[pl.BlockSpec((1, blk, 2 * LANES), lambda bi, h, qi: (bi, qi, h)),
                  pl.BlockSpec((1, s, 2 * LANES), lambda bi, h, qi: (bi, 0, h)),
                  pl.BlockSpec((1, s, LANES), lambda bi, h, qi: (bi, 0, h)),
                  pl.BlockSpec((1, blk, LANES), lambda bi, h, qi: (bi, qi, h))],
        out_specs=pl.BlockSpec((1, blk, LANES), lambda bi, h, qi: (bi, qi, h)),
        out_shape=jax.ShapeDtypeStruct((b, s, MLA_WIDTH), jnp.bfloat16),
        scratch_shapes=_attn_scratch(),
        compiler_params=pltpu.CompilerParams(
            dimension_semantics=("arbitrary",) * 3, vmem_limit_bytes=VMEM_LIMIT),
        name="mlaattn",
    )(mq, mk, mv, mg)


def _out_kernel(x_ref, od_ref, om_ref, p_ref, wo_ref, wp_ref, wg_ref, fg_ref, o_ref):
    f32 = jnp.float32
    bf16 = jnp.bfloat16
    h = (x_ref[...]
         + jnp.dot(od_ref[...], wo_ref[0:DIFF_WIDTH, :], preferred_element_type=f32)
         + jnp.dot(om_ref[...], wo_ref[DIFF_WIDTH:, :], preferred_element_type=f32))
    gate = jax.nn.sigmoid(jnp.dot(h.astype(bf16), wg_ref[...], preferred_element_type=f32))
    emb = jnp.dot(p_ref[...].astype(bf16), wp_ref[...], preferred_element_type=f32)
    h = h + emb * gate
    o_ref[...] = _rms(h, fg_ref[...])


def _out_call(x2, od, om, p2, wo, wp, wg, fg):
    t = x2.shape[0]
    bm = PROJ_BM
    row = lambda w: pl.BlockSpec((bm, w), lambda i: (i, 0))
    full = lambda a: pl.BlockSpec(a.shape, lambda i: (0, 0))
    return pl.pallas_call(
        _out_kernel,
        grid=(t // bm,),
        in_specs=[row(D_MODEL), row(DIFF_WIDTH), row(MLA_WIDTH), row(PLE_DIM),
                  full(wo), full(wp), full(wg), full(fg)],
        out_specs=row(D_MODEL),
        out_shape=jax.ShapeDtypeStruct((t, D_MODEL), jnp.float32),
        compiler_params=pltpu.CompilerParams(
            dimension_semantics=("arbitrary",), vmem_limit_bytes=VMEM_LIMIT),
        name="outproj",
    )(x2, od, om, p2, wo, wp, wg, fg)


def _rope_tables():
    f32 = jnp.float32
    lane = jnp.arange(LANES)
    inv_d = ROPE_THETA ** (-jnp.arange(0, DIFF_HD, 2, dtype=f32) / DIFF_HD)
    half_d = DIFF_HD // 2
    inv_d_l = inv_d[lane % half_d]
    sp_d = jnp.where(lane % DIFF_HD >= half_d, 1.0, 0.0)
    sm_d = jnp.where(lane % DIFF_HD < half_d, -1.0, 0.0)
    inv_m = ROPE_THETA ** (-jnp.arange(0, MLA_ROPE, 2, dtype=f32) / MLA_ROPE)
    half_m = MLA_ROPE // 2
    in_rope = (lane >= MLA_NOPE) & (lane < MLA_NOPE + MLA_ROPE)
    inv_m_l = jnp.where(in_rope, inv_m[(lane - MLA_NOPE) % half_m], 0.0)
    sp_m = jnp.where(in_rope & (lane >= MLA_NOPE + half_m), 1.0, 0.0)
    sm_m = jnp.where(in_rope & (lane < MLA_NOPE + half_m), -1.0, 0.0)
    zero = jnp.zeros((LANES,), f32)
    return jnp.stack([inv_d_l, sp_d, sm_d, inv_m_l, sp_m, sm_m, zero, zero]).astype(f32)


def kernel(x, p, positions, norm_g, w_in, diff_lambda, diff_subln_g, mla_q_norm_g, w_uq,
           mla_kv_norm_g, w_ukv, w_out, w_ple, w_ple_gate, final_norm_g):
    b, s, d = x.shape
    t = b * s
    f32 = jnp.float32
    bf16 = jnp.bfloat16
    i = 0

    offs = [0, 512, 1024, 1536, 2048, 2432, 2560, 2592, 3104]
    seg = [w_in[i][:, offs[j]:offs[j + 1]] for j in range(8)]
    wdq, wdk, wdv, wdg, wcq, wckv, wkr, wmg = seg
    zc = lambda n: jnp.zeros((d, n), f32)
    w_in_p = jnp.concatenate(
        [wdq, wdk, wdv, wdg, wcq, wckv, wmg, zc(MLA_NOPE), wkr, zc(LANES - MLA_NOPE - MLA_ROPE)],
        axis=1).astype(bf16)
    wuq_p = jnp.pad(w_uq[i].reshape(MLA_Q_LORA, MLA_HEADS, MLA_NOPE + MLA_ROPE),
                    ((0, 0), (0, 0), (0, LANES - MLA_NOPE - MLA_ROPE))
                    ).reshape(MLA_Q_LORA, MLA_HEADS * LANES).astype(bf16)
    wkv3 = w_ukv[i].reshape(MLA_KV_LORA, MLA_HEADS, MLA_NOPE + MLA_V)
    wk_p = jnp.pad(wkv3[:, :, :MLA_NOPE], ((0, 0), (0, 0), (0, LANES - MLA_NOPE))
                   ).reshape(MLA_KV_LORA, MLA_HEADS * LANES).astype(bf16)
    wv_p = wkv3[:, :, MLA_NOPE:].reshape(MLA_KV_LORA, MLA_WIDTH).astype(bf16)

    x2 = x.reshape(t, d)
    posb = jnp.broadcast_to(positions.astype(f32).reshape(t, 1), (t, LANES))
    dq, dk, dv, dg, mq, mk, mv, mg = _proj_call(
        x2, posb, _rope_tables(), norm_g[i].reshape(1, d), w_in_p,
        mla_q_norm_g[i].reshape(1, MLA_Q_LORA), wuq_p,
        mla_kv_norm_g[i].reshape(1, MLA_KV_LORA), wk_p, wv_p)

    r3 = lambda a: a.reshape(b, s, a.shape[-1])
    od = _diff_call(diff_lambda[i].astype(f32), r3(dq), r3(dk), r3(dv), r3(dg),
                    diff_subln_g[i].reshape(1, 2 * DIFF_HD))
    om = _mla_call(r3(mq), r3(mk), r3(mv), r3(mg))

    out = _out_call(x2, od.reshape(t, DIFF_WIDTH), om.reshape(t, MLA_WIDTH),
                    p[i].reshape(t, PLE_DIM), w_out[i].astype(bf16), w_ple[i].astype(bf16),
                    w_ple_gate[i].astype(bf16), final_norm_g.reshape(1, d))
    return out.reshape(b, s, d)
```

```python
import math
from functools import partial

import jax
import jax.numpy as jnp
from jax import lax
from jax.experimental import pallas as pl
from jax.experimental.pallas import tpu as pltpu

D_MODEL = 1024
PLE_DIM = 256
ROPE_THETA = 10000.0
RMS_EPS = 1e-6

DIFF_WIDTH = 512
DIFF_HEADS = 4
DIFF_HD = 64
MLA_WIDTH = 512
MLA_HEADS = 8
MLA_NOPE = 64
MLA_ROPE = 32
MLA_V = 64
MLA_Q_LORA = 384
MLA_KV_LORA = 128

LANES = 128
LOG2E = math.log2(math.e)
VMEM_LIMIT = 48 * 1024 * 1024

O_DQ, O_DK, O_DV, O_DG = 0, 512, 1024, 1536
O_CQ = 2048
O_CKV = O_CQ + MLA_Q_LORA
O_MG = O_CKV + MLA_KV_LORA
O_KR = O_MG + MLA_WIDTH
D_IN_P = O_KR + LANES

PROJ_BM = 512
ATT_BLK = 512


def _rms(x, g):
    return x * lax.rsqrt(jnp.mean(x * x, axis=-1, keepdims=True) + RMS_EPS) * g


def _silu(x):
    return x / (1.0 + jnp.exp(-x))


def _rope128(x, c, sp, sm, half):
    return (x * c + pltpu.roll(x, half, axis=1) * sp
            + pltpu.roll(x, LANES - half, axis=1) * sm)


def _proj_kernel(x_ref, pos_ref, tab_ref, ng_ref, w_ref, qg_ref, wuq_ref, kvg_ref,
                 wk_ref, wvt_ref, wdvt_ref,
                 dq_ref, dk_ref, dvt_ref, dg_ref, mq_ref, mk_ref, mvt_ref, mg_ref):
    f32 = jnp.float32
    bf16 = jnp.bfloat16
    x = x_ref[...]
    nb = _rms(x, ng_ref[...]).astype(bf16)

    pos = pos_ref[...]
    ang_d = pos * tab_ref[0:1, :]
    cd = jnp.cos(ang_d)
    sd = jnp.sin(ang_d)
    spd = sd * tab_ref[1:2, :]
    smd = sd * tab_ref[2:3, :]
    ang_m = pos * tab_ref[3:4, :]
    cm = jnp.cos(ang_m)
    s_m = jnp.sin(ang_m)
    spm = s_m * tab_ref[4:5, :]
    smm = s_m * tab_ref[5:6, :]

    def proj(off, width):
        return jnp.dot(nb, w_ref[:, off:off + width], preferred_element_type=f32)

    q_scale = DIFF_HD ** -0.5 * LOG2E
    dq = proj(O_DQ, DIFF_WIDTH)
    dk = proj(O_DK, DIFF_WIDTH)
    for c in range(DIFF_WIDTH // LANES):
        sl = slice(c * LANES, (c + 1) * LANES)
        dq_ref[:, sl] = (_rope128(dq[:, sl], cd, spd, smd, DIFF_HD // 2) * q_scale).astype(bf16)
        dk_ref[:, sl] = _rope128(dk[:, sl], cd, spd, smd, DIFF_HD // 2).astype(bf16)
    nt = (((1,), (1,)), ((), ()))
    dvt_ref[0] = lax.dot_general(wdvt_ref[...], nb, nt, preferred_element_type=f32).astype(bf16)
    dg_ref[...] = _silu(proj(O_DG, DIFF_WIDTH)).astype(bf16)
    mg_ref[...] = _silu(proj(O_MG, MLA_WIDTH)).astype(bf16)

    cq = proj(O_CQ, MLA_Q_LORA)
    cqn = _rms(cq, qg_ref[...]).astype(bf16)
    mq = jnp.dot(cqn, wuq_ref[...], preferred_element_type=f32)
    m_scale = (MLA_NOPE + MLA_ROPE) ** -0.5 * LOG2E

    ckv = proj(O_CKV, MLA_KV_LORA)
    ckvn = _rms(ckv, kvg_ref[...]).astype(bf16)
    kn = jnp.dot(ckvn, wk_ref[...], preferred_element_type=f32)
    mvt_ref[0] = lax.dot_general(wvt_ref[...], ckvn, nt, preferred_element_type=f32).astype(bf16)
    kr = _rope128(proj(O_KR, LANES), cm, spm, smm, MLA_ROPE // 2)
    for h in range(MLA_HEADS):
        sl = slice(h * LANES, (h + 1) * LANES)
        mq_ref[:, sl] = (_rope128(mq[:, sl], cm, spm, smm, MLA_ROPE // 2) * m_scale).astype(bf16)
        mk_ref[:, sl] = (kn[:, sl] + kr).astype(bf16)


def _proj_call(x2, posb, tab, ng, w_in_p, qg, wuq_p, kvg, wk_p, wvt_p, wdvt):
    t = x2.shape[0]
    bm = PROJ_BM
    assert bm == ATT_BLK
    bf16 = jnp.bfloat16
    row = lambda w: pl.BlockSpec((bm, w), lambda i: (i, 0))
    full = lambda a: pl.BlockSpec(a.shape, lambda i: (0, 0))
    tr = lambda w: pl.BlockSpec((1, w, bm), lambda i: (i, 0, 0))
    rows = lambda w: jax.ShapeDtypeStruct((t, w), bf16)
    trs = lambda w: jax.ShapeDtypeStruct((t // bm, w, bm), bf16)
    return pl.pallas_call(
        _proj_kernel,
        grid=(t // bm,),
        in_specs=[row(D_MODEL), row(LANES), full(tab), full(ng), full(w_in_p), full(qg),
                  full(wuq_p), full(kvg), full(wk_p), full(wvt_p), full(wdvt)],
        out_specs=[row(DIFF_WIDTH), row(DIFF_WIDTH), tr(DIFF_WIDTH), row(DIFF_WIDTH),
                   row(MLA_HEADS * LANES), row(MLA_HEADS * LANES), tr(MLA_WIDTH), row(MLA_WIDTH)],
        out_shape=[rows(DIFF_WIDTH), rows(DIFF_WIDTH), trs(DIFF_WIDTH), rows(DIFF_WIDTH),
                   rows(MLA_HEADS * LANES), rows(MLA_HEADS * LANES), trs(MLA_WIDTH), rows(MLA_WIDTH)],
        compiler_params=pltpu.CompilerParams(
            dimension_semantics=("arbitrary",), vmem_limit_bytes=VMEM_LIMIT),
        name="proj",
    )(x2, posb, tab, ng, w_in_p, qg, wuq_p, kvg, wk_p, wvt_p, wdvt)


def _attn_two_maps(get_q, get_k, vt_ref, acc_ref):
    f32 = jnp.float32
    blk = ATT_BLK
    qi = pl.program_id(2)
    acc_ref[...] = jnp.zeros(acc_ref.shape, f32)
    nt = (((1,), (1,)), ((), ()))

    def step(j, carry, masked):
        vt = vt_ref[0, j]
        out = []
        for i in range(2):
            m_prev, l_prev = carry[2 * i], carry[2 * i + 1]
            s = lax.dot_general(get_k(i, j), get_q(i), nt, preferred_element_type=f32)
            if masked:
                krow = lax.broadcasted_iota(jnp.int32, s.shape, 0)
                qcol = lax.broadcasted_iota(jnp.int32, s.shape, 1)
                s = jnp.where(krow <= qcol, s, -jnp.inf)
            m_new = jnp.maximum(m_prev, jnp.max(s, axis=0, keepdims=True))
            alpha = jnp.exp2(m_prev - m_new)
            p = jnp.exp2(s - m_new)
            l_new = alpha * l_prev + jnp.sum(p, axis=0, keepdims=True)
            acc_ref[i] = alpha * acc_ref[i] + jnp.dot(
                vt, p.astype(jnp.bfloat16), preferred_element_type=f32)
            out += [m_new, l_new]
        return tuple(out)

    neg = jnp.full((1, blk), -jnp.inf, f32)
    zero = jnp.zeros((1, blk), f32)
    carry = lax.fori_loop(0, qi, lambda j, c: step(j, c, False), (neg, zero, neg, zero))
    carry = step(qi, carry, True)
    return carry[1], carry[3]


def _diff_kernel(lam_ref, q_ref, k_ref, vt_ref, g_ref, sg_ref, o_ref, qz_ref, acc_ref):
    q = q_ref[0]
    lane = lax.broadcasted_iota(jnp.int32, q.shape, 1)
    zero = jnp.zeros_like(q)
    qz_ref[0] = jnp.where(lane < DIFF_HD, q, zero)
    qz_ref[1] = jnp.where(lane >= DIFF_HD, q, zero)

    l0, l1 = _attn_two_maps(
        lambda i: qz_ref[i],
        lambda i, j: k_ref[0, pl.ds(pl.multiple_of(j * ATT_BLK, ATT_BLK), ATT_BLK), :],
        vt_ref, acc_ref)

    lam_rows = lam_ref[...]
    lam_init = 0.8 - 0.6 * math.exp(-0.3 * 0)
    lam = (jnp.exp(jnp.sum(lam_rows[0:1] * lam_rows[1:2], axis=1, keepdims=True))
           - jnp.exp(jnp.sum(lam_rows[2:3] * lam_rows[3:4], axis=1, keepdims=True))
           + lam_init)
    ot = acc_ref[0] / l0 - lam * (acc_ref[1] / l1)
    o = _rms(ot.T, sg_ref[...]) * (1.0 - lam_init)
    o_ref[0] = (o * g_ref[0].astype(jnp.float32)).astype(o_ref.dtype)


def _mla_kernel(q_ref, k_ref, vt_ref, g_ref, o_ref, acc_ref):
    l0, l1 = _attn_two_maps(
        lambda i: q_ref[0, :, i * LANES:(i + 1) * LANES],
        lambda i, j: k_ref[0, pl.ds(pl.multiple_of(j * ATT_BLK, ATT_BLK), ATT_BLK),
                           i * LANES:(i + 1) * LANES],
        vt_ref, acc_ref)
    o0 = acc_ref[0] / l0
    o1 = acc_ref[1] / l1
    feat = lax.broadcasted_iota(jnp.int32, o0.shape, 0)
    o = jnp.where(feat < MLA_V, o0, o1).T
    o_ref[0] = (o * g_ref[0].astype(jnp.float32)).astype(o_ref.dtype)


def _attn_scratch():
    return [pltpu.VMEM((2, LANES, ATT_BLK), jnp.float32)]


def _diff_call(lam_rows, dq, dk, dvt, dg, subg):
    b, s, _ = dq.shape
    blk = ATT_BLK
    qspec = pl.BlockSpec((1, blk, LANES), lambda bi, h, qi: (bi, qi, h))
    kspec = pl.BlockSpec((1, s, LANES), lambda bi, h, qi: (bi, 0, h))
    vtspec = pl.BlockSpec((1, s // blk, LANES, blk), lambda bi, h, qi: (bi, 0, h, 0))
    return pl.pallas_call(
        _diff_kernel,
        grid=(b, DIFF_HEADS, s // blk),
        in_specs=[pl.BlockSpec(lam_rows.shape, lambda bi, h, qi: (0, 0)),
                  qspec, kspec, vtspec, qspec,
                  pl.BlockSpec(subg.shape, lambda bi, h, qi: (0, 0))],
        out_specs=qspec,
        out_shape=jax.ShapeDtypeStruct((b, s, DIFF_WIDTH), jnp.bfloat16),
        scratch_shapes=[pltpu.VMEM((2, blk, LANES), jnp.bfloat16)] + _attn_scratch(),
        compiler_params=pltpu.CompilerParams(
            dimension_semantics=("arbitrary",) * 3, vmem_limit_bytes=VMEM_LIMIT),
        name="diffattn",
    )(lam_rows, dq, dk, dvt, dg, subg)


def _mla_call(mq, mk, mvt, mg):
    b, s, _ = mq.shape
    blk = ATT_BLK
    pairs = MLA_HEADS // 2
    return pl.pallas_call(
        _mla_kernel,
        grid=(b, pairs, s // blk),
        in_specs=[pl.BlockSpec((1, blk, 2 * LANES), lambda bi, h, qi: (bi, qi, h)),
                  pl.BlockSpec((1, s, 2 * LANES), lambda bi, h, qi: (bi, 0, h)),
                  pl.BlockSpec((1, s // blk, LANES, blk), lambda bi, h, qi: (bi, 0, h, 0)),
                  pl.BlockSpec((1, blk, LANES), lambda bi, h, qi: (bi, qi, h))],
        out_specs=pl.BlockSpec((1, blk, LANES), lambda bi, h, qi: (bi, qi, h)),
        out_shape=jax.ShapeDtypeStruct((b, s, MLA_WIDTH), jnp.bfloat16),
        scratch_shapes=_attn_scratch(),
        compiler_params=pltpu.CompilerParams(
            dimension_semantics=("arbitrary",) * 3, vmem_limit_bytes=VMEM_LIMIT),
        name="mlaattn",
    )(mq, mk, mvt, mg)


def _out_kernel(x_ref, od_ref, om_ref, p_ref, wo_ref, wp_ref, wg_ref, fg_ref, o_ref):
    f32 = jnp.float32
    bf16 = jnp.bfloat16
    h = (x_ref[...]
         + jnp.dot(od_ref[...], wo_ref[0:DIFF_WIDTH, :], preferred_element_type=f32)
         + jnp.dot(om_ref[...], wo_ref[DIFF_WIDTH:, :], preferred_element_type=f32))
    gate = jax.nn.sigmoid(jnp.dot(h.astype(bf16), wg_ref[...], preferred_element_type=f32))
    emb = jnp.dot(p_ref[...].astype(bf16), wp_ref[...], preferred_element_type=f32)
    h = h + emb * gate
    o_ref[...] = _rms(h, fg_ref[...])


def _out_call(x2, od, om, p2, wo, wp, wg, fg):
    t = x2.shape[0]
    bm = PROJ_BM
    row = lambda w: pl.BlockSpec((bm, w), lambda i: (i, 0))
    full = lambda a: pl.BlockSpec(a.shape, lambda i: (0, 0))
    return pl.pallas_call(
        _out_kernel,
        grid=(t // bm,),
        in_specs=[row(D_MODEL), row(DIFF_WIDTH), row(MLA_WIDTH), row(PLE_DIM),
                  full(wo), full(wp), full(wg), full(fg)],
        out_specs=row(D_MODEL),
        out_shape=jax.ShapeDtypeStruct((t, D_MODEL), jnp.float32),
        compiler_params=pltpu.CompilerParams(
            dimension_semantics=("arbitrary",), vmem_limit_bytes=VMEM_LIMIT),
        name="outproj",
    )(x2, od, om, p2, wo, wp, wg, fg)


def _rope_tables():
    f32 = jnp.float32
    lane = jnp.arange(LANES)
    inv_d = ROPE_THETA ** (-jnp.arange(0, DIFF_HD, 2, dtype=f32) / DIFF_HD)
    half_d = DIFF_HD // 2
    inv_d_l = inv_d[lane % half_d]
    sp_d = jnp.where(lane % DIFF_HD >= half_d, 1.0, 0.0)
    sm_d = jnp.where(lane % DIFF_HD < half_d, -1.0, 0.0)
    inv_m = ROPE_THETA ** (-jnp.arange(0, MLA_ROPE, 2, dtype=f32) / MLA_ROPE)
    half_m = MLA_ROPE // 2
    in_rope = (lane >= MLA_NOPE) & (lane < MLA_NOPE + MLA_ROPE)
    inv_m_l = jnp.where(in_rope, inv_m[(lane - MLA_NOPE) % half_m], 0.0)
    sp_m = jnp.where(in_rope & (lane >= MLA_NOPE + half_m), 1.0, 0.0)
    sm_m = jnp.where(in_rope & (lane < MLA_NOPE + half_m), -1.0, 0.0)
    zero = jnp.zeros((LANES,), f32)
    return jnp.stack([inv_d_l, sp_d, sm_d, inv_m_l, sp_m, sm_m, zero, zero]).astype(f32)


def kernel(x, p, positions, norm_g, w_in, diff_lambda, diff_subln_g, mla_q_norm_g, w_uq,
           mla_kv_norm_g, w_ukv, w_out, w_ple, w_ple_gate, final_norm_g):
    b, s, d = x.shape
    t = b * s
    f32 = jnp.float32
    bf16 = jnp.bfloat16
    i = 0

    offs = [0, 512, 1024, 1536, 2048, 2432, 2560, 2592, 3104]
    seg = [w_in[i][:, offs[j]:offs[j + 1]] for j in range(8)]
    wdq, wdk, wdv, wdg, wcq, wckv, wkr, wmg = seg
    zc = lambda n: jnp.zeros((d, n), f32)
    w_in_p = jnp.concatenate(
        [wdq, wdk, wdv, wdg, wcq, wckv, wmg, zc(MLA_NOPE), wkr, zc(LANES - MLA_NOPE - MLA_ROPE)],
        axis=1).astype(bf16)
    wuq_p = jnp.pad(w_uq[i].reshape(MLA_Q_LORA, MLA_HEADS, MLA_NOPE + MLA_ROPE),
                    ((0, 0), (0, 0), (0, LANES - MLA_NOPE - MLA_ROPE))
                    ).reshape(MLA_Q_LORA, MLA_HEADS * LANES).astype(bf16)
    wkv3 = w_ukv[i].reshape(MLA_KV_LORA, MLA_HEADS, MLA_NOPE + MLA_V)
    wk_p = jnp.pad(wkv3[:, :, :MLA_NOPE], ((0, 0), (0, 0), (0, LANES - MLA_NOPE))
                   ).reshape(MLA_KV_LORA, MLA_HEADS * LANES).astype(bf16)
    wvt_p = wkv3[:, :, MLA_NOPE:].reshape(MLA_KV_LORA, MLA_WIDTH).T.astype(bf16)
    wdvt = wdv.T.astype(bf16)

    x2 = x.reshape(t, d)
    posb = jnp.broadcast_to(positions.astype(f32).reshape(t, 1), (t, LANES))
    dq, dk, dvt, dg, mq, mk, mvt, mg = _proj_call(
        x2, posb, _rope_tables(), norm_g[i].reshape(1, d), w_in_p,
        mla_q_norm_g[i].reshape(1, MLA_Q_LORA), wuq_p,
        mla_kv_norm_g[i].reshape(1, MLA_KV_LORA), wk_p, wvt_p, wdvt)

    r3 = lambda a: a.reshape(b, s, a.shape[-1])
    r4 = lambda a: a.reshape(b, s // ATT_BLK, a.shape[-2], ATT_BLK)
    od = _diff_call(diff_lambda[i].astype(f32), r3(dq), r3(dk), r4(dvt), r3(dg),
                    diff_subln_g[i].reshape(1, 2 * DIFF_HD))
    om = _mla_call(r3(mq), r3(mk), r4(mvt), r3(mg))

    out = _out_call(x2, od.reshape(t, DIFF_WIDTH), om.reshape(t, MLA_WIDTH),
                    p[i].reshape(t, PLE_DIM), w_out[i].astype(bf16), w_ple[i].astype(bf16),
                    w_ple_gate[i].astype(bf16), final_norm_g.reshape(1, d))
    return out.reshape(b, s, d)
```

```python
import math
from functools import partial

import jax
import jax.numpy as jnp
from jax import lax
from jax.experimental import pallas as pl
from jax.experimental.pallas import tpu as pltpu

D_MODEL = 1024
PLE_DIM = 256
ROPE_THETA = 10000.0
RMS_EPS = 1e-6

DIFF_WIDTH = 512
DIFF_HEADS = 4
DIFF_HD = 64
MLA_WIDTH = 512
MLA_HEADS = 8
MLA_NOPE = 64
MLA_ROPE = 32
MLA_V = 64
MLA_Q_LORA = 384
MLA_KV_LORA = 128

LANES = 128
LOG2E = math.log2(math.e)
VMEM_LIMIT = 48 * 1024 * 1024

O_DQ, O_DK, O_DV, O_DG = 0, 512, 1024, 1536
O_CQ = 2048
O_CKV = O_CQ + MLA_Q_LORA
O_MG = O_CKV + MLA_KV_LORA
O_KR = O_MG + MLA_WIDTH
D_IN_P = O_KR + LANES

PROJ_BM = 512
ATT_BLK = 512


def _rms(x, g):
    return x * lax.rsqrt(jnp.mean(x * x, axis=-1, keepdims=True) + RMS_EPS) * g


def _silu(x):
    return x / (1.0 + jnp.exp(-x))


def _rope128(x, c, sp, sm, half):
    return (x * c + pltpu.roll(x, half, axis=1) * sp
            + pltpu.roll(x, LANES - half, axis=1) * sm)


def _proj_kernel(x_ref, pos_ref, tab_ref, ng_ref, w_ref, qg_ref, wuq_ref, kvg_ref,
                 wk_ref, wvt_ref, wdvt_ref,
                 dq_ref, dk_ref, dvt_ref, dg_ref, mq_ref, mk_ref, mvt_ref, mg_ref):
    f32 = jnp.float32
    bf16 = jnp.bfloat16
    x = x_ref[...]
    nb = _rms(x, ng_ref[...]).astype(bf16)

    pos = pos_ref[...]
    ang_d = pos * tab_ref[0:1, :]
    cd = jnp.cos(ang_d)
    sd = jnp.sin(ang_d)
    spd = sd * tab_ref[1:2, :]
    smd = sd * tab_ref[2:3, :]
    ang_m = pos * tab_ref[3:4, :]
    cm = jnp.cos(ang_m)
    s_m = jnp.sin(ang_m)
    spm = s_m * tab_ref[4:5, :]
    smm = s_m * tab_ref[5:6, :]

    def proj(off, width):
        return jnp.dot(nb, w_ref[:, off:off + width], preferred_element_type=f32)

    q_scale = DIFF_HD ** -0.5 * LOG2E
    dq = proj(O_DQ, DIFF_WIDTH)
    dk = proj(O_DK, DIFF_WIDTH)
    for c in range(DIFF_WIDTH // LANES):
        sl = slice(c * LANES, (c + 1) * LANES)
        dq_ref[:, sl] = (_rope128(dq[:, sl], cd, spd, smd, DIFF_HD // 2) * q_scale).astype(bf16)
        dk_ref[:, sl] = _rope128(dk[:, sl], cd, spd, smd, DIFF_HD // 2).astype(bf16)
    nt = (((1,), (1,)), ((), ()))
    dvt_ref[0] = lax.dot_general(wdvt_ref[...], nb, nt, preferred_element_type=f32).astype(bf16)
    dg_ref[...] = _silu(proj(O_DG, DIFF_WIDTH)).astype(bf16)
    mg_ref[...] = _silu(proj(O_MG, MLA_WIDTH)).astype(bf16)

    cq = proj(O_CQ, MLA_Q_LORA)
    cqn = _rms(cq, qg_ref[...]).astype(bf16)
    mq = jnp.dot(cqn, wuq_ref[...], preferred_element_type=f32)
    m_scale = (MLA_NOPE + MLA_ROPE) ** -0.5 * LOG2E

    ckv = proj(O_CKV, MLA_KV_LORA)
    ckvn = _rms(ckv, kvg_ref[...]).astype(bf16)
    kn = jnp.dot(ckvn, wk_ref[...], preferred_element_type=f32)
    mvt_ref[0] = lax.dot_general(wvt_ref[...], ckvn, nt, preferred_element_type=f32).astype(bf16)
    kr = _rope128(proj(O_KR, LANES), cm, spm, smm, MLA_ROPE // 2)
    for h in range(MLA_HEADS):
        sl = slice(h * LANES, (h + 1) * LANES)
        mq_ref[:, sl] = (_rope128(mq[:, sl], cm, spm, smm, MLA_ROPE // 2) * m_scale).astype(bf16)
        mk_ref[:, sl] = (kn[:, sl] + kr).astype(bf16)


def _proj_call(x2, posb, tab, ng, w_in_p, qg, wuq_p, kvg, wk_p, wvt_p, wdvt):
    t = x2.shape[0]
    bm = PROJ_BM
    assert bm == ATT_BLK
    bf16 = jnp.bfloat16
    row = lambda w: pl.BlockSpec((bm, w), lambda i: (i, 0))
    full = lambda a: pl.BlockSpec(a.shape, lambda i: (0, 0))
    tr = lambda w: pl.BlockSpec((1, w, bm), lambda i: (i, 0, 0))
    rows = lambda w: jax.ShapeDtypeStruct((t, w), bf16)
    trs = lambda w: jax.ShapeDtypeStruct((t // bm, w, bm), bf16)
    return pl.pallas_call(
        _proj_kernel,
        grid=(t // bm,),
        in_specs=[row(D_MODEL), row(LANES), full(tab), full(ng), full(w_in_p), full(qg),
                  full(wuq_p), full(kvg), full(wk_p), full(wvt_p), full(wdvt)],
        out_specs=[row(DIFF_WIDTH), row(DIFF_WIDTH), tr(DIFF_WIDTH), row(DIFF_WIDTH),
                   row(MLA_HEADS * LANES), row(MLA_HEADS * LANES), tr(MLA_WIDTH), row(MLA_WIDTH)],
        out_shape=[rows(DIFF_WIDTH), rows(DIFF_WIDTH), trs(DIFF_WIDTH), rows(DIFF_WIDTH),
                   rows(MLA_HEADS * LANES), rows(MLA_HEADS * LANES), trs(MLA_WIDTH), rows(MLA_WIDTH)],
        compiler_params=pltpu.CompilerParams(
            dimension_semantics=("arbitrary",), vmem_limit_bytes=VMEM_LIMIT),
        name="proj",
    )(x2, posb, tab, ng, w_in_p, qg, wuq_p, kvg, wk_p, wvt_p, wdvt)


_NT = (((1,), (1,)), ((), ()))
SUM_LIMIT = 2.0 ** 64
ACC_LIMIT = 2.0 ** 120


def _causal_mask(s):
    krow = lax.broadcasted_iota(jnp.int32, s.shape, 0)
    qcol = lax.broadcasted_iota(jnp.int32, s.shape, 1)
    return jnp.where(krow <= qcol, s, -jnp.inf)


def _attn_two_maps(get_q, get_k, vt_ref, acc_ref, l_ref):
    f32 = jnp.float32
    blk = ATT_BLK
    qi = pl.program_id(2)
    acc_ref[...] = jnp.zeros(acc_ref.shape, f32)
    offs = [lax.dot_general(get_k(i, 0, 16), get_q(i), _NT, preferred_element_type=f32)[0:1, :]
            for i in range(2)]

    def run(units, sums):
        def qk(u):
            j, i, _ = units[u]
            return lax.dot_general(get_k(i, j, blk), get_q(i), _NT, preferred_element_type=f32)

        sums = list(sums)
        scores = {u: qk(u) for u in range(min(2, len(units)))}
        for u, (j, i, masked) in enumerate(units):
            s = scores.pop(u)
            if masked:
                s = _causal_mask(s)
            p = jnp.exp2(s - offs[i])
            sums[i] = sums[i] + jnp.sum(p, axis=0, keepdims=True)
            pb = p.astype(jnp.bfloat16)
            if u + 2 < len(units):
                scores[u + 2] = qk(u + 2)
            acc_ref[i] += jnp.dot(vt_ref[0, j], pb, preferred_element_type=f32)
        return tuple(sums)

    def both_maps(j, masked):
        return [(j, 0, masked), (j, 1, masked)]

    def finish(sums):
        l_ref[0] = sums[0]
        l_ref[1] = sums[1]

    zero = jnp.zeros((1, blk), f32)
    sums = lax.fori_loop(
        0, qi // 2,
        lambda jj, c: run(both_maps(2 * jj, False) + both_maps(2 * jj + 1, False), c),
        (zero, zero))

    @pl.when(qi % 2 == 1)
    def _():
        finish(run(both_maps(qi - 1, False) + both_maps(qi, True), sums))

    @pl.when(qi % 2 == 0)
    def _():
        finish(run(both_maps(qi, True), sums))

    sums = (l_ref[0], l_ref[1])
    ok = jnp.logical_and(sums[0] < SUM_LIMIT, sums[1] < SUM_LIMIT)
    n_bad = (jnp.sum(jnp.where(ok, 0.0, 1.0))
             + jnp.sum(jnp.where(jnp.abs(acc_ref[0]) < ACC_LIMIT, 0.0, 1.0))
             + jnp.sum(jnp.where(jnp.abs(acc_ref[1]) < ACC_LIMIT, 0.0, 1.0)))

    @pl.when(n_bad > 0.0)
    def _():
        _attn_two_maps_rescaling(get_q, get_k, vt_ref, acc_ref, l_ref)


def _attn_two_maps_rescaling(get_q, get_k, vt_ref, acc_ref, l_ref):
    f32 = jnp.float32
    blk = ATT_BLK
    qi = pl.program_id(2)
    acc_ref[...] = jnp.zeros(acc_ref.shape, f32)
    nt = _NT

    def step(j, carry, masked):
        vt = vt_ref[0, j]
        out = []
        for i in range(2):
            m_prev, l_prev = carry[2 * i], carry[2 * i + 1]
            s = lax.dot_general(get_k(i, j, blk), get_q(i), nt, preferred_element_type=f32)
            if masked:
                s = _causal_mask(s)
            m_new = jnp.maximum(m_prev, jnp.max(s, axis=0, keepdims=True))
            alpha = jnp.exp2(m_prev - m_new)
            p = jnp.exp2(s - m_new)
            l_new = alpha * l_prev + jnp.sum(p, axis=0, keepdims=True)
            acc_ref[i] = alpha * acc_ref[i] + jnp.dot(
                vt, p.astype(jnp.bfloat16), preferred_element_type=f32)
            out += [m_new, l_new]
        return tuple(out)

    neg = jnp.full((1, blk), -jnp.inf, f32)
    zero = jnp.zeros((1, blk), f32)
    carry = lax.fori_loop(0, qi, lambda j, c: step(j, c, False), (neg, zero, neg, zero))
    carry = step(qi, carry, True)
    l_ref[0] = carry[1]
    l_ref[1] = carry[3]


def _diff_kernel(lam_ref, q_ref, k_ref, vt_ref, g_ref, sg_ref, o_ref, qz_ref, acc_ref, l_ref):
    q = q_ref[0]
    lane = lax.broadcasted_iota(jnp.int32, q.shape, 1)
    zero = jnp.zeros_like(q)
    qz_ref[0] = jnp.where(lane < DIFF_HD, q, zero)
    qz_ref[1] = jnp.where(lane >= DIFF_HD, q, zero)

    _attn_two_maps(
        lambda i: qz_ref[i],
        lambda i, j, rows: k_ref[0, pl.ds(pl.multiple_of(j * ATT_BLK, ATT_BLK), rows), :],
        vt_ref, acc_ref, l_ref)
    l0, l1 = l_ref[0], l_ref[1]

    lam_rows = lam_ref[...]
    lam_init = 0.8 - 0.6 * math.exp(-0.3 * 0)
    lam = (jnp.exp(jnp.sum(lam_rows[0:1] * lam_rows[1:2], axis=1, keepdims=True))
           - jnp.exp(jnp.sum(lam_rows[2:3] * lam_rows[3:4], axis=1, keepdims=True))
           + lam_init)
    ot = acc_ref[0] / l0 - lam * (acc_ref[1] / l1)
    o = _rms(ot.T, sg_ref[...]) * (1.0 - lam_init)
    o_ref[0] = (o * g_ref[0].astype(jnp.float32)).astype(o_ref.dtype)


def _mla_kernel(q_ref, k_ref, vt_ref, g_ref, o_ref, acc_ref, l_ref):
    _attn_two_maps(
        lambda i: q_ref[0, :, i * LANES:(i + 1) * LANES],
        lambda i, j, rows: k_ref[0, pl.ds(pl.multiple_of(j * ATT_BLK, ATT_BLK), rows),
                                 i * LANES:(i + 1) * LANES],
        vt_ref, acc_ref, l_ref)
    o0 = acc_ref[0] / l_ref[0]
    o1 = acc_ref[1] / l_ref[1]
    feat = lax.broadcasted_iota(jnp.int32, o0.shape, 0)
    o = jnp.where(feat < MLA_V, o0, o1).T
    o_ref[0] = (o * g_ref[0].astype(jnp.float32)).astype(o_ref.dtype)


def _attn_scratch():
    return [pltpu.VMEM((2, LANES, ATT_BLK), jnp.float32),
            pltpu.VMEM((2, 1, ATT_BLK), jnp.float32)]


def _diff_call(lam_rows, dq, dk, dvt, dg, subg):
    b, s, _ = dq.shape
    blk = ATT_BLK
    qspec = pl.BlockSpec((1, blk, LANES), lambda bi, h, qi: (bi, qi, h))
    kspec = pl.BlockSpec((1, s, LANES), lambda bi, h, qi: (bi, 0, h))
    vtspec = pl.BlockSpec((1, s // blk, LANES, blk), lambda bi, h, qi: (bi, 0, h, 0))
    return pl.pallas_call(
        _diff_kernel,
        grid=(b, DIFF_HEADS, s // blk),
        in_specs=[pl.BlockSpec(lam_rows.shape, lambda bi, h, qi: (0, 0)),
                  qspec, kspec, vtspec, qspec,
                  pl.BlockSpec(subg.shape, lambda bi, h, qi: (0, 0))],
        out_specs=qspec,
        out_shape=jax.ShapeDtypeStruct((b, s, DIFF_WIDTH), jnp.bfloat16),
        scratch_shapes=[pltpu.VMEM((2, blk, LANES), jnp.bfloat16)] + _attn_scratch(),
        compiler_params=pltpu.CompilerParams(
            dimension_semantics=("arbitrary",) * 3, vmem_limit_bytes=VMEM_LIMIT),
        name="diffattn",
    )(lam_rows, dq, dk, dvt, dg, subg)


def _mla_call(mq, mk, mvt, mg):
    b, s, _ = mq.shape
    blk = ATT_BLK
    pairs = MLA_HEADS // 2
    return pl.pallas_call(
        _mla_kernel,
        grid=(b, pairs, s // blk),
        in_specs=[pl.BlockSpec((1, blk, 2 * LANES), lambda bi, h, qi: (bi, qi, h)),
                  pl.BlockSpec((1, s, 2 * LANES), lambda bi, h, qi: (bi, 0, h)),
                  pl.BlockSpec((1, s // blk, LANES, blk), lambda bi, h, qi: (bi, 0, h, 0)),
                  pl.BlockSpec((1, blk, LANES), lambda bi, h, qi: (bi, qi, h))],
        out_specs=pl.BlockSpec((1, blk, LANES), lambda bi, h, qi: (bi, qi, h)),
        out_shape=jax.ShapeDtypeStruct((b, s, MLA_WIDTH), jnp.bfloat16),
        scratch_shapes=_attn_scratch(),
        compiler_params=pltpu.CompilerParams(
            dimension_semantics=("arbitrary",) * 3, vmem_limit_bytes=VMEM_LIMIT),
        name="mlaattn",
    )(mq, mk, mvt, mg)


def _out_kernel(x_ref, od_ref, om_ref, p_ref, wo_ref, wp_ref, wg_ref, fg_ref, o_ref):
    f32 = jnp.float32
    bf16 = jnp.bfloat16
    h = (x_ref[...]
         + jnp.dot(od_ref[...], wo_ref[0:DIFF_WIDTH, :], preferred_element_type=f32)
         + jnp.dot(om_ref[...], wo_ref[DIFF_WIDTH:, :], preferred_element_type=f32))
    gate = jax.nn.sigmoid(jnp.dot(h.astype(bf16), wg_ref[...], preferred_element_type=f32))
    emb = jnp.dot(p_ref[...].astype(bf16), wp_ref[...], preferred_element_type=f32)
    h = h + emb * gate
    o_ref[...] = _rms(h, fg_ref[...])


def _out_call(x2, od, om, p2, wo, wp, wg, fg):
    t = x2.shape[0]
    bm = PROJ_BM
    row = lambda w: pl.BlockSpec((bm, w), lambda i: (i, 0))
    full = lambda a: pl.BlockSpec(a.shape, lambda i: (0, 0))
    return pl.pallas_call(
        _out_kernel,
        grid=(t // bm,),
        in_specs=[row(D_MODEL), row(DIFF_WIDTH), row(MLA_WIDTH), row(PLE_DIM),
                  full(wo), full(wp), full(wg), full(fg)],
        out_specs=row(D_MODEL),
        out_shape=jax.ShapeDtypeStruct((t, D_MODEL), jnp.float32),
        compiler_params=pltpu.CompilerParams(
            dimension_semantics=("arbitrary",), vmem_limit_bytes=VMEM_LIMIT),
        name="outproj",
    )(x2, od, om, p2, wo, wp, wg, fg)


def _rope_tables():
    f32 = jnp.float32
    lane = jnp.arange(LANES)
    inv_d = ROPE_THETA ** (-jnp.arange(0, DIFF_HD, 2, dtype=f32) / DIFF_HD)
    half_d = DIFF_HD // 2
    inv_d_l = inv_d[lane % half_d]
    sp_d = jnp.where(lane % DIFF_HD >= half_d, 1.0, 0.0)
    sm_d = jnp.where(lane % DIFF_HD < half_d, -1.0, 0.0)
    inv_m = ROPE_THETA ** (-jnp.arange(0, MLA_ROPE, 2, dtype=f32) / MLA_ROPE)
    half_m = MLA_ROPE // 2
    in_rope = (lane >= MLA_NOPE) & (lane < MLA_NOPE + MLA_ROPE)
    inv_m_l = jnp.where(in_rope, inv_m[(lane - MLA_NOPE) % half_m], 0.0)
    sp_m = jnp.where(in_rope & (lane >= MLA_NOPE + half_m), 1.0, 0.0)
    sm_m = jnp.where(in_rope & (lane < MLA_NOPE + half_m), -1.0, 0.0)
    zero = jnp.zeros((LANES,), f32)
    return jnp.stack([inv_d_l, sp_d, sm_d, inv_m_l, sp_m, sm_m, zero, zero]).astype(f32)


def kernel(x, p, positions, norm_g, w_in, diff_lambda, diff_subln_g, mla_q_norm_g, w_uq,
           mla_kv_norm_g, w_ukv, w_out, w_ple, w_ple_gate, final_norm_g):
    b, s, d = x.shape
    t = b * s
    f32 = jnp.float32
    bf16 = jnp.bfloat16
    i = 0

    offs = [0, 512, 1024, 1536, 2048, 2432, 2560, 2592, 3104]
    seg = [w_in[i][:, offs[j]:offs[j + 1]] for j in range(8)]
    wdq, wdk, wdv, wdg, wcq, wckv, wkr, wmg = seg
    zc = lambda n: jnp.zeros((d, n), f32)
    w_in_p = jnp.concatenate(
        [wdq, wdk, wdv, wdg, wcq, wckv, wmg, zc(MLA_NOPE), wkr, zc(LANES - MLA_NOPE - MLA_ROPE)],
        axis=1).astype(bf16)
    wuq_p = jnp.pad(w_uq[i].reshape(MLA_Q_LORA, MLA_HEADS, MLA_NOPE + MLA_ROPE),
                    ((0, 0), (0, 0), (0, LANES - MLA_NOPE - MLA_ROPE))
                    ).reshape(MLA_Q_LORA, MLA_HEADS * LANES).astype(bf16)
    wkv3 = w_ukv[i].reshape(MLA_KV_LORA, MLA_HEADS, MLA_NOPE + MLA_V)
    wk_p = jnp.pad(wkv3[:, :, :MLA_NOPE], ((0, 0), (0, 0), (0, LANES - MLA_NOPE))
                   ).reshape(MLA_KV_LORA, MLA_HEADS * LANES).astype(bf16)
    wvt_p = wkv3[:, :, MLA_NOPE:].reshape(MLA_KV_LORA, MLA_WIDTH).T.astype(bf16)
    wdvt = wdv.T.astype(bf16)

    x2 = x.reshape(t, d)
    posb = jnp.broadcast_to(positions.astype(f32).reshape(t, 1), (t, LANES))
    dq, dk, dvt, dg, mq, mk, mvt, mg = _proj_call(
        x2, posb, _rope_tables(), norm_g[i].reshape(1, d), w_in_p,
        mla_q_norm_g[i].reshape(1, MLA_Q_LORA), wuq_p,
        mla_kv_norm_g[i].reshape(1, MLA_KV_LORA), wk_p, wvt_p, wdvt)

    r3 = lambda a: a.reshape(b, s, a.shape[-1])
    r4 = lambda a: a.reshape(b, s // ATT_BLK, a.shape[-2], ATT_BLK)
    od = _diff_call(diff_lambda[i].astype(f32), r3(dq), r3(dk), r4(dvt), r3(dg),
                    diff_subln_g[i].reshape(1, 2 * DIFF_HD))
    om = _mla_call(r3(mq), r3(mk), r4(mvt), r3(mg))

    out = _out_call(x2, od.reshape(t, DIFF_WIDTH), om.reshape(t, MLA_WIDTH),
                    p[i].reshape(t, PLE_DIM), w_out[i].astype(bf16), w_ple[i].astype(bf16),
                    w_ple_gate[i].astype(bf16), final_norm_g.reshape(1, d))
    return out.reshape(b, s, d)
```

```python
import math
from functools import partial

import jax
import jax.numpy as jnp
from jax import lax
from jax.experimental import pallas as pl
from jax.experimental.pallas import tpu as pltpu

D_MODEL = 1024
PLE_DIM = 256
ROPE_THETA = 10000.0
RMS_EPS = 1e-6

DIFF_WIDTH = 512
DIFF_HEADS = 4
DIFF_HD = 64
MLA_WIDTH = 512
MLA_HEADS = 8
MLA_NOPE = 64
MLA_ROPE = 32
MLA_V = 64
MLA_Q_LORA = 384
MLA_KV_LORA = 128

LANES = 128
LOG2E = math.log2(math.e)
VMEM_LIMIT = 48 * 1024 * 1024

O_DQ, O_DK, O_DV, O_DG = 0, 512, 1024, 1536
O_CQ = 2048
O_CKV = O_CQ + MLA_Q_LORA
O_MG = O_CKV + MLA_KV_LORA
O_KR = O_MG + MLA_WIDTH
D_IN_P = O_KR + LANES

PROJ_BM = 512
ATT_BLK = 512


def _rms(x, g):
    return x * lax.rsqrt(jnp.mean(x * x, axis=-1, keepdims=True) + RMS_EPS) * g


def _silu(x):
    return x / (1.0 + jnp.exp(-x))


def _rope128(x, c, sp, sm, half):
    return (x * c + pltpu.roll(x, half, axis=1) * sp
            + pltpu.roll(x, LANES - half, axis=1) * sm)


def _proj_kernel(x_ref, pos_ref, tab_ref, ng_ref, w_ref, qg_ref, wuq_ref, kvg_ref,
                 wk_ref, wvt_ref, wdvt_ref,
                 dq_ref, dk_ref, dvt_ref, dg_ref, mq_ref, mk_ref, mvt_ref, mg_ref):
    f32 = jnp.float32
    bf16 = jnp.bfloat16
    x = x_ref[...]
    nb = _rms(x, ng_ref[...]).astype(bf16)

    pos = pos_ref[...]
    ang_d = pos * tab_ref[0:1, :]
    cd = jnp.cos(ang_d)
    sd = jnp.sin(ang_d)
    spd = sd * tab_ref[1:2, :]
    smd = sd * tab_ref[2:3, :]
    ang_m = pos * tab_ref[3:4, :]
    cm = jnp.cos(ang_m)
    s_m = jnp.sin(ang_m)
    spm = s_m * tab_ref[4:5, :]
    smm = s_m * tab_ref[5:6, :]

    def proj(off, width):
        return jnp.dot(nb, w_ref[:, off:off + width], preferred_element_type=f32)

    q_scale = DIFF_HD ** -0.5 * LOG2E
    dq = proj(O_DQ, DIFF_WIDTH)
    dk = proj(O_DK, DIFF_WIDTH)
    for c in range(DIFF_WIDTH // LANES):
        sl = slice(c * LANES, (c + 1) * LANES)
        dq_ref[:, sl] = (_rope128(dq[:, sl], cd, spd, smd, DIFF_HD // 2) * q_scale).astype(bf16)
        dk_ref[:, sl] = _rope128(dk[:, sl], cd, spd, smd, DIFF_HD // 2).astype(bf16)
    nt = (((1,), (1,)), ((), ()))
    dvt_ref[0] = lax.dot_general(wdvt_ref[...], nb, nt, preferred_element_type=f32).astype(bf16)
    dg_ref[...] = _silu(proj(O_DG, DIFF_WIDTH)).astype(bf16)
    mg_ref[...] = _silu(proj(O_MG, MLA_WIDTH)).astype(bf16)

    cq = proj(O_CQ, MLA_Q_LORA)
    cqn = _rms(cq, qg_ref[...]).astype(bf16)
    mq = jnp.dot(cqn, wuq_ref[...], preferred_element_type=f32)
    m_scale = (MLA_NOPE + MLA_ROPE) ** -0.5 * LOG2E

    ckv = proj(O_CKV, MLA_KV_LORA)
    ckvn = _rms(ckv, kvg_ref[...]).astype(bf16)
    kn = jnp.dot(ckvn, wk_ref[...], preferred_element_type=f32)
    mvt_ref[0] = lax.dot_general(wvt_ref[...], ckvn, nt, preferred_element_type=f32).astype(bf16)
    kr = _rope128(proj(O_KR, LANES), cm, spm, smm, MLA_ROPE // 2)
    for h in range(MLA_HEADS):
        sl = slice(h * LANES, (h + 1) * LANES)
        mq_ref[:, sl] = (_rope128(mq[:, sl], cm, spm, smm, MLA_ROPE // 2) * m_scale).astype(bf16)
        mk_ref[:, sl] = (kn[:, sl] + kr).astype(bf16)


def _proj_call(x2, posb, tab, ng, w_in_p, qg, wuq_p, kvg, wk_p, wvt_p, wdvt):
    t = x2.shape[0]
    bm = PROJ_BM
    assert bm == ATT_BLK
    bf16 = jnp.bfloat16
    row = lambda w: pl.BlockSpec((bm, w), lambda i: (i, 0))
    full = lambda a: pl.BlockSpec(a.shape, lambda i: (0, 0))
    tr = lambda w: pl.BlockSpec((1, w, bm), lambda i: (i, 0, 0))
    rows = lambda w: jax.ShapeDtypeStruct((t, w), bf16)
    trs = lambda w: jax.ShapeDtypeStruct((t // bm, w, bm), bf16)
    return pl.pallas_call(
        _proj_kernel,
        grid=(t // bm,),
        in_specs=[row(D_MODEL), row(LANES), full(tab), full(ng), full(w_in_p), full(qg),
                  full(wuq_p), full(kvg), full(wk_p), full(wvt_p), full(wdvt)],
        out_specs=[row(DIFF_WIDTH), row(DIFF_WIDTH), tr(DIFF_WIDTH), row(DIFF_WIDTH),
                   row(MLA_HEADS * LANES), row(MLA_HEADS * LANES), tr(MLA_WIDTH), row(MLA_WIDTH)],
        out_shape=[rows(DIFF_WIDTH), rows(DIFF_WIDTH), trs(DIFF_WIDTH), rows(DIFF_WIDTH),
                   rows(MLA_HEADS * LANES), rows(MLA_HEADS * LANES), trs(MLA_WIDTH), rows(MLA_WIDTH)],
        compiler_params=pltpu.CompilerParams(
            dimension_semantics=("arbitrary",), vmem_limit_bytes=VMEM_LIMIT),
        name="proj",
    )(x2, posb, tab, ng, w_in_p, qg, wuq_p, kvg, wk_p, wvt_p, wdvt)


_NT = (((1,), (1,)), ((), ()))
HALF = ATT_BLK // 2
SUM_LO, SUM_HI = 2.0 ** -64, 2.0 ** 64
OUT_LIMIT = 2.0 ** 100


def _attn_two_maps(get_q, get_k, vt_ref, acc_ref, l_ref):
    f32 = jnp.float32
    blk = ATT_BLK
    qi = pl.program_id(2)
    acc_ref[...] = jnp.zeros(acc_ref.shape, f32)

    def run(units, sums):
        def qk(u):
            j, i, k0, nk, q0, nq, _ = units[u]
            return lax.dot_general(get_k(i, j, k0, nk), get_q(i, q0, nq), _NT,
                                   preferred_element_type=f32)

        sums = list(sums)
        scores = {u: qk(u) for u in range(min(2, len(units)))}
        for u, (j, i, k0, nk, q0, nq, masked) in enumerate(units):
            s = scores.pop(u)
            if masked:
                krow = lax.broadcasted_iota(jnp.int32, s.shape, 0) + k0
                qcol = lax.broadcasted_iota(jnp.int32, s.shape, 1) + q0
                s = jnp.where(krow <= qcol, s, -jnp.inf)
            p = jnp.exp2(s)
            col = jnp.sum(p, axis=0, keepdims=True)
            for h in range(q0 // HALF, (q0 + nq) // HALF):
                sums[2 * i + h] = sums[2 * i + h] + col[:, h * HALF - q0:(h + 1) * HALF - q0]
            pb = p.astype(jnp.bfloat16)
            if u + 2 < len(units):
                scores[u + 2] = qk(u + 2)
            acc_ref[i, :, q0:q0 + nq] += jnp.dot(vt_ref[0, j, :, k0:k0 + nk], pb,
                                                 preferred_element_type=f32)
        return tuple(sums)

    def full(j):
        return [(j, 0, 0, blk, 0, blk, False), (j, 1, 0, blk, 0, blk, False)]

    def diagonal(j):
        return ([(j, i, 0, HALF, 0, blk, True) for i in range(2)]
                + [(j, i, HALF, HALF, HALF, HALF, True) for i in range(2)])

    def finish(sums):
        for i in range(2):
            l_ref[i, :, 0:HALF] = sums[2 * i]
            l_ref[i, :, HALF:blk] = sums[2 * i + 1]

    zero = jnp.zeros((1, HALF), f32)
    sums = lax.fori_loop(0, qi // 2, lambda jj, c: run(full(2 * jj) + full(2 * jj + 1), c),
                         (zero,) * 4)

    @pl.when(qi % 2 == 1)
    def _():
        finish(run(full(qi - 1) + diagonal(qi), sums))

    @pl.when(qi % 2 == 0)
    def _():
        finish(run(diagonal(qi), sums))


def _attn_two_maps_rescaling(get_q, get_k, vt_ref, acc_ref, l_ref):
    f32 = jnp.float32
    blk = ATT_BLK
    qi = pl.program_id(2)
    acc_ref[...] = jnp.zeros(acc_ref.shape, f32)

    def step(j, carry, masked):
        vt = vt_ref[0, j]
        out = []
        for i in range(2):
            m_prev, l_prev = carry[2 * i], carry[2 * i + 1]
            s = lax.dot_general(get_k(i, j, 0, blk), get_q(i, 0, blk), _NT,
                                preferred_element_type=f32)
            if masked:
                krow = lax.broadcasted_iota(jnp.int32, s.shape, 0)
                qcol = lax.broadcasted_iota(jnp.int32, s.shape, 1)
                s = jnp.where(krow <= qcol, s, -jnp.inf)
            m_new = jnp.maximum(m_prev, jnp.max(s, axis=0, keepdims=True))
            alpha = jnp.exp2(m_prev - m_new)
            p = jnp.exp2(s - m_new)
            l_new = alpha * l_prev + jnp.sum(p, axis=0, keepdims=True)
            acc_ref[i] = alpha * acc_ref[i] + jnp.dot(
                vt, p.astype(jnp.bfloat16), preferred_element_type=f32)
            out += [m_new, l_new]
        return tuple(out)

    neg = jnp.full((1, blk), -jnp.inf, f32)
    zero = jnp.zeros((1, blk), f32)
    carry = lax.fori_loop(0, qi, lambda j, c: step(j, c, False), (neg, zero, neg, zero))
    carry = step(qi, carry, True)
    l_ref[0] = carry[1]
    l_ref[1] = carry[3]


def _sums_out_of_range(l0, l1):
    ok = ((l0 > SUM_LO) & (l0 < SUM_HI)) & ((l1 > SUM_LO) & (l1 < SUM_HI))
    return jnp.where(ok, 0.0, 1.0)


def _guarded(get_q, get_k, vt_ref, acc_ref, l_ref, epilogue):
    _attn_two_maps(get_q, get_k, vt_ref, acc_ref, l_ref)
    n_bad = epilogue()

    @pl.when(n_bad > 0.0)
    def _():
        _attn_two_maps_rescaling(get_q, get_k, vt_ref, acc_ref, l_ref)
        epilogue()


def _diff_kernel(lam_ref, q_ref, k_ref, vt_ref, g_ref, sg_ref, o_ref, qz_ref, acc_ref, l_ref):
    f32 = jnp.float32
    q = q_ref[0]
    lane = lax.broadcasted_iota(jnp.int32, q.shape, 1)
    zero = jnp.zeros_like(q)
    qz_ref[0] = jnp.where(lane < DIFF_HD, q, zero)
    qz_ref[1] = jnp.where(lane >= DIFF_HD, q, zero)

    lam_rows = lam_ref[...]
    lam_init = 0.8 - 0.6 * math.exp(-0.3 * 0)
    lam = (jnp.exp(jnp.sum(lam_rows[0:1] * lam_rows[1:2], axis=1, keepdims=True))
           - jnp.exp(jnp.sum(lam_rows[2:3] * lam_rows[3:4], axis=1, keepdims=True))
           + lam_init)

    def epilogue():
        l0, l1 = l_ref[0], l_ref[1]
        ot = acc_ref[0] * (1.0 / l0) - acc_ref[1] * (lam / l1)
        ms = jnp.mean(ot * ot, axis=0, keepdims=True)
        ot = ot * lax.rsqrt(ms + RMS_EPS)
        o = ot.T * (sg_ref[...] * (1.0 - lam_init)) * g_ref[0].astype(f32)
        o_ref[0] = o.astype(o_ref.dtype)
        return jnp.sum(_sums_out_of_range(l0, l1) + jnp.where(ms < OUT_LIMIT, 0.0, 1.0))

    _guarded(lambda i, q0, n: qz_ref[i, q0:q0 + n, :],
             lambda i, j, k0, n: k_ref[0, pl.ds(pl.multiple_of(j * ATT_BLK + k0, HALF), n), :],
             vt_ref, acc_ref, l_ref, epilogue)


def _mla_kernel(q_ref, k_ref, vt_ref, g_ref, o_ref, acc_ref, l_ref):
    f32 = jnp.float32

    def epilogue():
        l0, l1 = l_ref[0], l_ref[1]
        feat = lax.broadcasted_iota(jnp.int32, (LANES, ATT_BLK), 0)
        ot = jnp.where(feat < MLA_V, acc_ref[0] * (1.0 / l0), acc_ref[1] * (1.0 / l1))
        peak = jnp.max(jnp.abs(ot), axis=0, keepdims=True)
        o_ref[0] = (ot.T * g_ref[0].astype(f32)).astype(o_ref.dtype)
        return jnp.sum(_sums_out_of_range(l0, l1) + jnp.where(peak < OUT_LIMIT, 0.0, 1.0))

    _guarded(lambda i, q0, n: q_ref[0, q0:q0 + n, i * LANES:(i + 1) * LANES],
             lambda i, j, k0, n: k_ref[0, pl.ds(pl.multiple_of(j * ATT_BLK + k0, HALF), n),
                                       i * LANES:(i + 1) * LANES],
             vt_ref, acc_ref, l_ref, epilogue)


def _attn_scratch():
    return [pltpu.VMEM((2, LANES, ATT_BLK), jnp.float32),
            pltpu.VMEM((2, 1, ATT_BLK), jnp.float32)]


def _diff_call(lam_rows, dq, dk, dvt, dg, subg):
    b, s, _ = dq.shape
    blk = ATT_BLK
    qspec = pl.BlockSpec((1, blk, LANES), lambda bi, h, qi: (bi, qi, h))
    kspec = pl.BlockSpec((1, s, LANES), lambda bi, h, qi: (bi, 0, h))
    vtspec = pl.BlockSpec((1, s // blk, LANES, blk), lambda bi, h, qi: (bi, 0, h, 0))
    return pl.pallas_call(
        _diff_kernel,
        grid=(b, DIFF_HEADS, s // blk),
        in_specs=[pl.BlockSpec(lam_rows.shape, lambda bi, h, qi: (0, 0)),
                  qspec, kspec, vtspec, qspec,
                  pl.BlockSpec(subg.shape, lambda bi, h, qi: (0, 0))],
        out_specs=qspec,
        out_shape=jax.ShapeDtypeStruct((b, s, DIFF_WIDTH), jnp.bfloat16),
        scratch_shapes=[pltpu.VMEM((2, blk, LANES), jnp.bfloat16)] + _attn_scratch(),
        compiler_params=pltpu.CompilerParams(
            dimension_semantics=("arbitrary",) * 3, vmem_limit_bytes=VMEM_LIMIT),
        name="diffattn",
    )(lam_rows, dq, dk, dvt, dg, subg)


def _mla_call(mq, mk, mvt, mg):
    b, s, _ = mq.shape
    blk = ATT_BLK
    pairs = MLA_HEADS // 2
    return pl.pallas_call(
        _mla_kernel,
        grid=(b, pairs, s // blk),
        in_specs=[pl.BlockSpec((1, blk, 2 * LANES), lambda bi, h, qi: (bi, qi, h)),
                  pl.BlockSpec((1, s, 2 * LANES), lambda bi, h, qi: (bi, 0, h)),
                  pl.BlockSpec((1, s // blk, LANES, blk), lambda bi, h, qi: (bi, 0, h, 0)),
                  pl.BlockSpec((1, blk, LANES), lambda bi, h, qi: (bi, qi, h))],
        out_specs=pl.BlockSpec((1, blk, LANES), lambda bi, h, qi: (bi, qi, h)),
        out_shape=jax.ShapeDtypeStruct((b, s, MLA_WIDTH), jnp.bfloat16),
        scratch_shapes=_attn_scratch(),
        compiler_params=pltpu.CompilerParams(
            dimension_semantics=("arbitrary",) * 3, vmem_limit_bytes=VMEM_LIMIT),
        name="mlaattn",
    )(mq, mk, mvt, mg)


def _out_kernel(x_ref, od_ref, om_ref, p_ref, wo_ref, wp_ref, wg_ref, fg_ref, o_ref):
    f32 = jnp.float32
    bf16 = jnp.bfloat16
    h = (x_ref[...]
         + jnp.dot(od_ref[...], wo_ref[0:DIFF_WIDTH, :], preferred_element_type=f32)
         + jnp.dot(om_ref[...], wo_ref[DIFF_WIDTH:, :], preferred_element_type=f32))
    gate = jax.nn.sigmoid(jnp.dot(h.astype(bf16), wg_ref[...], preferred_element_type=f32))
    emb = jnp.dot(p_ref[...].astype(bf16), wp_ref[...], preferred_element_type=f32)
    h = h + emb * gate
    o_ref[...] = _rms(h, fg_ref[...])


def _out_call(x2, od, om, p2, wo, wp, wg, fg):
    t = x2.shape[0]
    bm = PROJ_BM
    row = lambda w: pl.BlockSpec((bm, w), lambda i: (i, 0))
    full = lambda a: pl.BlockSpec(a.shape, lambda i: (0, 0))
    return pl.pallas_call(
        _out_kernel,
        grid=(t // bm,),
        in_specs=[row(D_MODEL), row(DIFF_WIDTH), row(MLA_WIDTH), row(PLE_DIM),
                  full(wo), full(wp), full(wg), full(fg)],
        out_specs=row(D_MODEL),
        out_shape=jax.ShapeDtypeStruct((t, D_MODEL), jnp.float32),
        compiler_params=pltpu.CompilerParams(
            dimension_semantics=("arbitrary",), vmem_limit_bytes=VMEM_LIMIT),
        name="outproj",
    )(x2, od, om, p2, wo, wp, wg, fg)


def _rope_tables():
    f32 = jnp.float32
    lane = jnp.arange(LANES)
    inv_d = ROPE_THETA ** (-jnp.arange(0, DIFF_HD, 2, dtype=f32) / DIFF_HD)
    half_d = DIFF_HD // 2
    inv_d_l = inv_d[lane % half_d]
    sp_d = jnp.where(lane % DIFF_HD >= half_d, 1.0, 0.0)
    sm_d = jnp.where(lane % DIFF_HD < half_d, -1.0, 0.0)
    inv_m = ROPE_THETA ** (-jnp.arange(0, MLA_ROPE, 2, dtype=f32) / MLA_ROPE)
    half_m = MLA_ROPE // 2
    in_rope = (lane >= MLA_NOPE) & (lane < MLA_NOPE + MLA_ROPE)
    inv_m_l = jnp.where(in_rope, inv_m[(lane - MLA_NOPE) % half_m], 0.0)
    sp_m = jnp.where(in_rope & (lane >= MLA_NOPE + half_m), 1.0, 0.0)
    sm_m = jnp.where(in_rope & (lane < MLA_NOPE + half_m), -1.0, 0.0)
    zero = jnp.zeros((LANES,), f32)
    return jnp.stack([inv_d_l, sp_d, sm_d, inv_m_l, sp_m, sm_m, zero, zero]).astype(f32)


def kernel(x, p, positions, norm_g, w_in, diff_lambda, diff_subln_g, mla_q_norm_g, w_uq,
           mla_kv_norm_g, w_ukv, w_out, w_ple, w_ple_gate, final_norm_g):
    b, s, d = x.shape
    t = b * s
    f32 = jnp.float32
    bf16 = jnp.bfloat16
    i = 0

    offs = [0, 512, 1024, 1536, 2048, 2432, 2560, 2592, 3104]
    seg = [w_in[i][:, offs[j]:offs[j + 1]] for j in range(8)]
    wdq, wdk, wdv, wdg, wcq, wckv, wkr, wmg = seg
    zc = lambda n: jnp.zeros((d, n), f32)
    w_in_p = jnp.concatenate(
        [wdq, wdk, wdv, wdg, wcq, wckv, wmg, zc(MLA_NOPE), wkr, zc(LANES - MLA_NOPE - MLA_ROPE)],
        axis=1).astype(bf16)
    wuq_p = jnp.pad(w_uq[i].reshape(MLA_Q_LORA, MLA_HEADS, MLA_NOPE + MLA_ROPE),
                    ((0, 0), (0, 0), (0, LANES - MLA_NOPE - MLA_ROPE))
                    ).reshape(MLA_Q_LORA, MLA_HEADS * LANES).astype(bf16)
    wkv3 = w_ukv[i].reshape(MLA_KV_LORA, MLA_HEADS, MLA_NOPE + MLA_V)
    wk_p = jnp.pad(wkv3[:, :, :MLA_NOPE], ((0, 0), (0, 0), (0, LANES - MLA_NOPE))
                   ).reshape(MLA_KV_LORA, MLA_HEADS * LANES).astype(bf16)
    wvt_p = wkv3[:, :, MLA_NOPE:].reshape(MLA_KV_LORA, MLA_WIDTH).T.astype(bf16)
    wdvt = wdv.T.astype(bf16)

    x2 = x.reshape(t, d)
    posb = jnp.broadcast_to(positions.astype(f32).reshape(t, 1), (t, LANES))
    dq, dk, dvt, dg, mq, mk, mvt, mg = _proj_call(
        x2, posb, _rope_tables(), norm_g[i].reshape(1, d), w_in_p,
        mla_q_norm_g[i].reshape(1, MLA_Q_LORA), wuq_p,
        mla_kv_norm_g[i].reshape(1, MLA_KV_LORA), wk_p, wvt_p, wdvt)

    r3 = lambda a: a.reshape(b, s, a.shape[-1])
    r4 = lambda a: a.reshape(b, s // ATT_BLK, a.shape[-2], ATT_BLK)
    od = _diff_call(diff_lambda[i].astype(f32), r3(dq), r3(dk), r4(dvt), r3(dg),
                    diff_subln_g[i].reshape(1, 2 * DIFF_HD))
    om = _mla_call(r3(mq), r3(mk), r4(mvt), r3(mg))

    out = _out_call(x2, od.reshape(t, DIFF_WIDTH), om.reshape(t, MLA_WIDTH),
                    p[i].reshape(t, PLE_DIM), w_out[i].astype(bf16), w_ple[i].astype(bf16),
                    w_ple_gate[i].astype(bf16), final_norm_g.reshape(1, d))
    return out.reshape(b, s, d)
```

```python
import functools
import math

import jax
import jax.numpy as jnp
from jax import lax
from jax.experimental import pallas as pl
from jax.experimental.pallas import tpu as pltpu

D_MODEL = 1024
PLE_DIM = 256
ROPE_THETA = 10000.0
RMS_EPS = 1e-6

DIFF_WIDTH = 512
DIFF_HEADS = 4
DIFF_HD = 64
MLA_WIDTH = 512
MLA_HEADS = 8
MLA_NOPE = 64
MLA_ROPE = 32
MLA_V = 64
MLA_Q_LORA = 384
MLA_KV_LORA = 128

LANES = 128
LOG2E = math.log2(math.e)
VMEM_LIMIT = 48 * 1024 * 1024

O_DQ, O_DK, O_DG = 0, 512, 1024
O_CQ = 1536
O_CKV = O_CQ + MLA_Q_LORA
O_MG = O_CKV + MLA_KV_LORA
O_KR = O_MG + MLA_WIDTH
D_IN_P = O_KR + LANES

BLK = 512
HALF = BLK // 2


def _rms(x, g):
    return x * lax.rsqrt(jnp.mean(x * x, axis=-1, keepdims=True) + RMS_EPS) * g


def _silu(x):
    return x / (1.0 + jnp.exp(-x))


def _rope128(x, c, sp, sm, half):
    return (x * c + pltpu.roll(x, half, axis=1) * sp
            + pltpu.roll(x, LANES - half, axis=1) * sm)


def _folded(c, n):
    return jnp.where(c < n // 2, c, n + n // 2 - 1 - c)


def _proj_kernel(x_ref, pos_ref, tab_ref, ng_ref, w_ref, qg_ref, wuq_ref, kvg_ref,
                 wk_ref, wvt_ref, wdvt_ref,
                 dq_ref, dk_ref, dvt_ref, dg_ref, mq_ref, mk_ref, mvt_ref, mg_ref):
    f32 = jnp.float32
    bf16 = jnp.bfloat16
    x = x_ref[...]
    nb = _rms(x, ng_ref[...]).astype(bf16)

    pos = pos_ref[...]
    ang_d = pos * tab_ref[0:1, :]
    cd = jnp.cos(ang_d)
    sd = jnp.sin(ang_d)
    spd = sd * tab_ref[1:2, :]
    smd = sd * tab_ref[2:3, :]
    ang_m = pos * tab_ref[3:4, :]
    cm = jnp.cos(ang_m)
    s_m = jnp.sin(ang_m)
    spm = s_m * tab_ref[4:5, :]
    smm = s_m * tab_ref[5:6, :]

    def proj(off, width):
        return jnp.dot(nb, w_ref[:, off:off + width], preferred_element_type=f32)

    q_scale = DIFF_HD ** -0.5 * LOG2E
    dq = proj(O_DQ, DIFF_WIDTH)
    dk = proj(O_DK, DIFF_WIDTH)
    for c in range(DIFF_WIDTH // LANES):
        sl = slice(c * LANES, (c + 1) * LANES)
        dq_ref[0, :, sl] = (_rope128(dq[:, sl], cd, spd, smd, DIFF_HD // 2) * q_scale).astype(bf16)
        dk_ref[:, sl] = _rope128(dk[:, sl], cd, spd, smd, DIFF_HD // 2).astype(bf16)
    nt = (((1,), (1,)), ((), ()))
    dvt_ref[0] = lax.dot_general(wdvt_ref[...], nb, nt, preferred_element_type=f32).astype(bf16)
    dg_ref[0] = _silu(proj(O_DG, DIFF_WIDTH)).astype(bf16)
    mg_ref[0] = _silu(proj(O_MG, MLA_WIDTH)).astype(bf16)

    cq = proj(O_CQ, MLA_Q_LORA)
    cqn = _rms(cq, qg_ref[...]).astype(bf16)
    mq = jnp.dot(cqn, wuq_ref[...], preferred_element_type=f32)
    m_scale = (MLA_NOPE + MLA_ROPE) ** -0.5 * LOG2E

    ckv = proj(O_CKV, MLA_KV_LORA)
    ckvn = _rms(ckv, kvg_ref[...]).astype(bf16)
    kn = jnp.dot(ckvn, wk_ref[...], preferred_element_type=f32)
    mvt_ref[0] = lax.dot_general(wvt_ref[...], ckvn, nt, preferred_element_type=f32).astype(bf16)
    kr = _rope128(proj(O_KR, LANES), cm, spm, smm, MLA_ROPE // 2)
    for h in range(MLA_HEADS):
        sl = slice(h * LANES, (h + 1) * LANES)
        mq_ref[0, :, sl] = (_rope128(mq[:, sl], cm, spm, smm, MLA_ROPE // 2) * m_scale).astype(bf16)
        mk_ref[:, sl] = (kn[:, sl] + kr).astype(bf16)


def _proj_call(x2, posb, tab, ng, w_in_p, qg, wuq_p, kvg, wk_p, wvt_p, wdvt, nq):
    t = x2.shape[0]
    bf16 = jnp.bfloat16
    row = lambda w: pl.BlockSpec((BLK, w), lambda i: (i, 0))
    full = lambda a: pl.BlockSpec(a.shape, lambda i: (0, 0))
    blk3 = lambda w: pl.BlockSpec((1, w, BLK), lambda i: (i, 0, 0))
    fold = lambda w: pl.BlockSpec((1, BLK, w), lambda i: (i // nq * nq + _folded(i % nq, nq), 0, 0))
    rows = lambda w: jax.ShapeDtypeStruct((t, w), bf16)
    blks = lambda w: jax.ShapeDtypeStruct((t // BLK, w, BLK), bf16)
    folds = lambda w: jax.ShapeDtypeStruct((t // BLK, BLK, w), bf16)
    return pl.pallas_call(
        _proj_kernel,
        grid=(t // BLK,),
        in_specs=[row(D_MODEL), row(LANES), full(tab), full(ng), full(w_in_p), full(qg),
                  full(wuq_p), full(kvg), full(wk_p), full(wvt_p), full(wdvt)],
        out_specs=[fold(DIFF_WIDTH), row(DIFF_WIDTH), blk3(DIFF_WIDTH), fold(DIFF_WIDTH),
                   fold(MLA_HEADS * LANES), row(MLA_HEADS * LANES), blk3(MLA_WIDTH),
                   fold(MLA_WIDTH)],
        out_shape=[folds(DIFF_WIDTH), rows(DIFF_WIDTH), blks(DIFF_WIDTH), folds(DIFF_WIDTH),
                   folds(MLA_HEADS * LANES), rows(MLA_HEADS * LANES), blks(MLA_WIDTH),
                   folds(MLA_WIDTH)],
        compiler_params=pltpu.CompilerParams(
            dimension_semantics=("arbitrary",), vmem_limit_bytes=VMEM_LIMIT),
        name="proj",
    )(x2, posb, tab, ng, w_in_p, qg, wuq_p, kvg, wk_p, wvt_p, wdvt)


_NT = (((1,), (1,)), ((), ()))
SUM_LO, SUM_HI = 2.0 ** -64, 2.0 ** 64
OUT_LIMIT = 2.0 ** 100


def _attn_fast(nq, get_q, get_k, vt_ref, acc_ref, l_ref, epilogue):
    f32 = jnp.float32
    c = pl.program_id(2)
    acc_ref[...] = jnp.zeros(acc_ref.shape, f32)
    l_ref[...] = jnp.zeros(l_ref.shape, f32)

    def diagonal(side, j):
        return ([(side, j, i, 0, HALF, 0, BLK, True) for i in range(2)]
                + [(side, j, i, HALF, HALF, HALF, HALF, True) for i in range(2)])

    units = diagonal(0, c)
    for t in range(nq - 1):
        on_side0 = t < c
        side = jnp.where(on_side0, 0, 1)
        j = jnp.where(on_side0, t, t - c)
        units += [(side, j, i, 0, BLK, 0, BLK, False) for i in range(2)]
        if t == nq // 2 - 2:
            side0_done = len(units)
    units += diagonal(1, nq - 1 - c)
    bad0 = None

    def qk(u):
        side, j, i, k0, nk, q0, nq_, _ = units[u]
        return lax.dot_general(get_k(i, j, k0, nk), get_q(side, i, q0, nq_), _NT,
                               preferred_element_type=f32)

    scores = {u: qk(u) for u in range(2)}
    for u, (side, j, i, k0, nk, q0, nq_, masked) in enumerate(units):
        s = scores.pop(u)
        if masked:
            krow = lax.broadcasted_iota(jnp.int32, s.shape, 0) + k0
            qcol = lax.broadcasted_iota(jnp.int32, s.shape, 1) + q0
            s = jnp.where(krow <= qcol, s, -jnp.inf)
        p = jnp.exp2(s)
        l_ref[side, i, :, q0:q0 + nq_] += jnp.sum(p, axis=0, keepdims=True)
        pb = p.astype(jnp.bfloat16)
        if u + 2 < len(units):
            scores[u + 2] = qk(u + 2)
        acc_ref[side, i, :, q0:q0 + nq_] += jnp.dot(vt_ref[0, j, :, k0:k0 + nk], pb,
                                                    preferred_element_type=f32)
        if u + 1 == side0_done:
            bad0 = epilogue(0)
    return bad0 + epilogue(1)


def _attn_rescaling(nq, get_q, get_k, vt_ref, acc_ref, l_ref):
    f32 = jnp.float32
    c = pl.program_id(2)
    acc_ref[...] = jnp.zeros(acc_ref.shape, f32)

    for side, n_full in ((0, c), (1, nq - 1 - c)):
        def step(j, carry, masked, side=side):
            vt = vt_ref[0, j]
            out = []
            for i in range(2):
                m_prev, l_prev = carry[2 * i], carry[2 * i + 1]
                s = lax.dot_general(get_k(i, j, 0, BLK), get_q(side, i, 0, BLK), _NT,
                                    preferred_element_type=f32)
                if masked:
                    krow = lax.broadcasted_iota(jnp.int32, s.shape, 0)
                    qcol = lax.broadcasted_iota(jnp.int32, s.shape, 1)
                    s = jnp.where(krow <= qcol, s, -jnp.inf)
                m_new = jnp.maximum(m_prev, jnp.max(s, axis=0, keepdims=True))
                alpha = jnp.exp2(m_prev - m_new)
                p = jnp.exp2(s - m_new)
                l_new = alpha * l_prev + jnp.sum(p, axis=0, keepdims=True)
                acc_ref[side, i] = alpha * acc_ref[side, i] + jnp.dot(
                    vt, p.astype(jnp.bfloat16), preferred_element_type=f32)
                out += [m_new, l_new]
            return tuple(out)

        neg = jnp.full((1, BLK), -jnp.inf, f32)
        zero = jnp.zeros((1, BLK), f32)
        carry = lax.fori_loop(0, n_full, lambda j, cr: step(j, cr, False),
                              (neg, zero, neg, zero))
        carry = step(n_full, carry, True)
        l_ref[side, 0] = carry[1]
        l_ref[side, 1] = carry[3]


def _sums_out_of_range(l0, l1):
    ok = ((l0 > SUM_LO) & (l0 < SUM_HI)) & ((l1 > SUM_LO) & (l1 < SUM_HI))
    return jnp.where(ok, 0.0, 1.0)


def _guarded(nq, get_q, get_k, vt_ref, acc_ref, l_ref, epilogue):
    n_bad = jnp.sum(_attn_fast(nq, get_q, get_k, vt_ref, acc_ref, l_ref, epilogue))

    @pl.when(n_bad > 0.0)
    def _():
        _attn_rescaling(nq, get_q, get_k, vt_ref, acc_ref, l_ref)
        epilogue(0)
        epilogue(1)


def _key_rows(j, k0, n):
    return pl.ds(pl.multiple_of(j * BLK + k0, HALF), n)


def _diff_kernel(nq, lam_ref, q_ref, k_ref, vt_ref, g_ref, sg_ref, o_ref, qz_ref, acc_ref, l_ref):
    f32 = jnp.float32
    for side in range(2):
        q = q_ref[0, side, 0]
        lane = lax.broadcasted_iota(jnp.int32, q.shape, 1)
        zero = jnp.zeros_like(q)
        qz_ref[side, 0] = jnp.where(lane < DIFF_HD, q, zero)
        qz_ref[side, 1] = jnp.where(lane >= DIFF_HD, q, zero)

    lam_rows = lam_ref[...]
    lam_init = 0.8 - 0.6 * math.exp(-0.3 * 0)
    lam = (jnp.exp(jnp.sum(lam_rows[0:1] * lam_rows[1:2], axis=1, keepdims=True))
           - jnp.exp(jnp.sum(lam_rows[2:3] * lam_rows[3:4], axis=1, keepdims=True))
           + lam_init)

    def epilogue(side):
        l0, l1 = l_ref[side, 0], l_ref[side, 1]
        ot = acc_ref[side, 0] * (1.0 / l0) - acc_ref[side, 1] * (lam / l1)
        ms = jnp.mean(ot * ot, axis=0, keepdims=True)
        ot = ot * lax.rsqrt(ms + RMS_EPS)
        o = ot.T * (sg_ref[...] * (1.0 - lam_init)) * g_ref[0, side, 0].astype(f32)
        o_ref[0, side, 0] = o.astype(o_ref.dtype)
        return _sums_out_of_range(l0, l1) + jnp.where(ms < OUT_LIMIT, 0.0, 1.0)

    _guarded(nq,
             lambda side, i, q0, n: qz_ref[side, i, q0:q0 + n, :],
             lambda i, j, k0, n: k_ref[0, _key_rows(j, k0, n), :],
             vt_ref, acc_ref, l_ref, epilogue)


def _mla_kernel(nq, q_ref, k_ref, vt_ref, g_ref, o_ref, acc_ref, l_ref):
    f32 = jnp.float32

    def epilogue(side):
        l0, l1 = l_ref[side, 0], l_ref[side, 1]
        feat = lax.broadcasted_iota(jnp.int32, (LANES, BLK), 0)
        ot = jnp.where(feat < MLA_V, acc_ref[side, 0] * (1.0 / l0), acc_ref[side, 1] * (1.0 / l1))
        peak = jnp.max(jnp.abs(ot), axis=0, keepdims=True)
        o_ref[0, side, 0] = (ot.T * g_ref[0, side, 0].astype(f32)).astype(o_ref.dtype)
        return _sums_out_of_range(l0, l1) + jnp.where(peak < OUT_LIMIT, 0.0, 1.0)

    _guarded(nq,
             lambda side, i, q0, n: q_ref[0, side, 0, q0:q0 + n, i * LANES:(i + 1) * LANES],
             lambda i, j, k0, n: k_ref[0, _key_rows(j, k0, n), i * LANES:(i + 1) * LANES],
             vt_ref, acc_ref, l_ref, epilogue)


def _attn_scratch():
    return [pltpu.VMEM((2, 2, LANES, BLK), jnp.float32),
            pltpu.VMEM((2, 2, 1, BLK), jnp.float32)]


def _pair_spec(width):
    return pl.BlockSpec((1, 2, 1, BLK, width), lambda bi, h, c: (bi, 0, c, 0, h))


def _diff_call(lam_rows, dq, dk, dvt, dg, subg):
    b, _, npair, _, _ = dq.shape
    nq = 2 * npair
    s = nq * BLK
    kspec = pl.BlockSpec((1, s, LANES), lambda bi, h, c: (bi, 0, h))
    vtspec = pl.BlockSpec((1, nq, LANES, BLK), lambda bi, h, c: (bi, 0, h, 0))
    return pl.pallas_call(
        functools.partial(_diff_kernel, nq),
        grid=(b, DIFF_HEADS, npair),
        in_specs=[pl.BlockSpec(lam_rows.shape, lambda bi, h, c: (0, 0)),
                  _pair_spec(LANES), kspec, vtspec, _pair_spec(LANES),
                  pl.BlockSpec(subg.shape, lambda bi, h, c: (0, 0))],
        out_specs=_pair_spec(LANES),
        out_shape=jax.ShapeDtypeStruct(dq.shape, jnp.bfloat16),
        scratch_shapes=[pltpu.VMEM((2, 2, BLK, LANES), jnp.bfloat16)] + _attn_scratch(),
        compiler_params=pltpu.CompilerParams(
            dimension_semantics=("arbitrary",) * 3, vmem_limit_bytes=VMEM_LIMIT),
        name="diffattn",
    )(lam_rows, dq, dk, dvt, dg, subg)


def _mla_call(mq, mk, mvt, mg):
    b, _, npair, _, _ = mq.shape
    nq = 2 * npair
    s = nq * BLK
    return pl.pallas_call(
        functools.partial(_mla_kernel, nq),
        grid=(b, MLA_HEADS // 2, npair),
        in_specs=[_pair_spec(2 * LANES),
                  pl.BlockSpec((1, s, 2 * LANES), lambda bi, h, c: (bi, 0, h)),
                  pl.BlockSpec((1, nq, LANES, BLK), lambda bi, h, c: (bi, 0, h, 0)),
                  _pair_spec(LANES)],
        out_specs=_pair_spec(LANES),
        out_shape=jax.ShapeDtypeStruct(mg.shape, jnp.bfloat16),
        scratch_shapes=_attn_scratch(),
        compiler_params=pltpu.CompilerParams(
            dimension_semantics=("arbitrary",) * 3, vmem_limit_bytes=VMEM_LIMIT),
        name="mlaattn",
    )(mq, mk, mvt, mg)


def _out_kernel(x_ref, od_ref, om_ref, p_ref, wo_ref, wp_ref, wg_ref, fg_ref, o_ref):
    f32 = jnp.float32
    bf16 = jnp.bfloat16
    h = (x_ref[...]
         + jnp.dot(od_ref[0], wo_ref[0:DIFF_WIDTH, :], preferred_element_type=f32)
         + jnp.dot(om_ref[0], wo_ref[DIFF_WIDTH:, :], preferred_element_type=f32))
    gate = jax.nn.sigmoid(jnp.dot(h.astype(bf16), wg_ref[...], preferred_element_type=f32))
    emb = jnp.dot(p_ref[...].astype(bf16), wp_ref[...], preferred_element_type=f32)
    h = h + emb * gate
    o_ref[...] = _rms(h, fg_ref[...])


def _out_call(x2, od, om, p2, wo, wp, wg, fg, nq):
    t = x2.shape[0]
    row = lambda w: pl.BlockSpec((BLK, w), lambda i: (i, 0))
    full = lambda a: pl.BlockSpec(a.shape, lambda i: (0, 0))
    fold = lambda w: pl.BlockSpec((1, BLK, w), lambda i: (i // nq * nq + _folded(i % nq, nq), 0, 0))
    return pl.pallas_call(
        _out_kernel,
        grid=(t // BLK,),
        in_specs=[row(D_MODEL), fold(DIFF_WIDTH), fold(MLA_WIDTH), row(PLE_DIM),
                  full(wo), full(wp), full(wg), full(fg)],
        out_specs=row(D_MODEL),
        out_shape=jax.ShapeDtypeStruct((t, D_MODEL), jnp.float32),
        compiler_params=pltpu.CompilerParams(
            dimension_semantics=("arbitrary",), vmem_limit_bytes=VMEM_LIMIT),
        name="outproj",
    )(x2, od, om, p2, wo, wp, wg, fg)


def _rope_tables():
    f32 = jnp.float32
    lane = jnp.arange(LANES)
    inv_d = ROPE_THETA ** (-jnp.arange(0, DIFF_HD, 2, dtype=f32) / DIFF_HD)
    half_d = DIFF_HD // 2
    inv_d_l = inv_d[lane % half_d]
    sp_d = jnp.where(lane % DIFF_HD >= half_d, 1.0, 0.0)
    sm_d = jnp.where(lane % DIFF_HD < half_d, -1.0, 0.0)
    inv_m = ROPE_THETA ** (-jnp.arange(0, MLA_ROPE, 2, dtype=f32) / MLA_ROPE)
    half_m = MLA_ROPE // 2
    in_rope = (lane >= MLA_NOPE) & (lane < MLA_NOPE + MLA_ROPE)
    inv_m_l = jnp.where(in_rope, inv_m[(lane - MLA_NOPE) % half_m], 0.0)
    sp_m = jnp.where(in_rope & (lane >= MLA_NOPE + half_m), 1.0, 0.0)
    sm_m = jnp.where(in_rope & (lane < MLA_NOPE + half_m), -1.0, 0.0)
    zero = jnp.zeros((LANES,), f32)
    return jnp.stack([inv_d_l, sp_d, sm_d, inv_m_l, sp_m, sm_m, zero, zero]).astype(f32)


def kernel(x, p, positions, norm_g, w_in, diff_lambda, diff_subln_g, mla_q_norm_g, w_uq,
           mla_kv_norm_g, w_ukv, w_out, w_ple, w_ple_gate, final_norm_g):
    b, s, d = x.shape
    t = b * s
    nq = s // BLK
    assert s % (2 * BLK) == 0
    f32 = jnp.float32
    bf16 = jnp.bfloat16
    i = 0

    offs = [0, 512, 1024, 1536, 2048, 2432, 2560, 2592, 3104]
    seg = [w_in[i][:, offs[j]:offs[j + 1]] for j in range(8)]
    wdq, wdk, wdv, wdg, wcq, wckv, wkr, wmg = seg
    zc = lambda n: jnp.zeros((d, n), f32)
    w_in_p = jnp.concatenate(
        [wdq, wdk, wdg, wcq, wckv, wmg, zc(MLA_NOPE), wkr, zc(LANES - MLA_NOPE - MLA_ROPE)],
        axis=1).astype(bf16)
    wuq_p = jnp.pad(w_uq[i].reshape(MLA_Q_LORA, MLA_HEADS, MLA_NOPE + MLA_ROPE),
                    ((0, 0), (0, 0), (0, LANES - MLA_NOPE - MLA_ROPE))
                    ).reshape(MLA_Q_LORA, MLA_HEADS * LANES).astype(bf16)
    wkv3 = w_ukv[i].reshape(MLA_KV_LORA, MLA_HEADS, MLA_NOPE + MLA_V)
    wk_p = jnp.pad(wkv3[:, :, :MLA_NOPE], ((0, 0), (0, 0), (0, LANES - MLA_NOPE))
                   ).reshape(MLA_KV_LORA, MLA_HEADS * LANES).astype(bf16)
    wvt_p = wkv3[:, :, MLA_NOPE:].reshape(MLA_KV_LORA, MLA_WIDTH).T.astype(bf16)
    wdvt = wdv.T.astype(bf16)

    x2 = x.reshape(t, d)
    posb = jnp.broadcast_to(positions.astype(f32).reshape(t, 1), (t, LANES))
    dq, dk, dvt, dg, mq, mk, mvt, mg = _proj_call(
        x2, posb, _rope_tables(), norm_g[i].reshape(1, d), w_in_p,
        mla_q_norm_g[i].reshape(1, MLA_Q_LORA), wuq_p,
        mla_kv_norm_g[i].reshape(1, MLA_KV_LORA), wk_p, wvt_p, wdvt, nq)

    r3 = lambda a: a.reshape(b, s, a.shape[-1])
    r4 = lambda a: a.reshape(b, nq, a.shape[-2], BLK)
    r5 = lambda a: a.reshape(b, 2, nq // 2, BLK, a.shape[-1])
    od = _diff_call(diff_lambda[i].astype(f32), r5(dq), r3(dk), r4(dvt), r5(dg),
                    diff_subln_g[i].reshape(1, 2 * DIFF_HD))
    om = _mla_call(r5(mq), r3(mk), r4(mvt), r5(mg))

    out = _out_call(x2, od.reshape(t // BLK, BLK, DIFF_WIDTH), om.reshape(t // BLK, BLK, MLA_WIDTH),
                    p[i].reshape(t, PLE_DIM), w_out[i].astype(bf16), w_ple[i].astype(bf16),
                    w_ple_gate[i].astype(bf16), final_norm_g.reshape(1, d), nq)
    return out.reshape(b, s, d)
```

```python
import functools
import math

import jax
import jax.numpy as jnp
from jax import lax
from jax.experimental import pallas as pl
from jax.experimental.pallas import tpu as pltpu

D_MODEL = 1024
PLE_DIM = 256
ROPE_THETA = 10000.0
RMS_EPS = 1e-6

DIFF_WIDTH = 512
DIFF_HEADS = 4
DIFF_HD = 64
MLA_WIDTH = 512
MLA_HEADS = 8
MLA_NOPE = 64
MLA_ROPE = 32
MLA_V = 64
MLA_Q_LORA = 384
MLA_KV_LORA = 128

LANES = 128
LOG2E = math.log2(math.e)
VMEM_LIMIT = 48 * 1024 * 1024

O_DQ, O_DK, O_DG, O_MG = 0, 512, 1024, 1536
O_CQ = 2048
O_CKV = O_CQ + MLA_Q_LORA
O_KR = O_CKV + MLA_KV_LORA
D_IN_P = O_KR + LANES

_NT = (((1,), (1,)), ((), ()))
BLK = 512
HALF = BLK // 2


def _rms(x, g):
    return x * lax.rsqrt(jnp.mean(x * x, axis=-1, keepdims=True) + RMS_EPS) * g


def _silu(x):
    return x / (1.0 + jnp.exp(-x))


def _rope128(x, c, sp, sm, half):
    return (x * c + pltpu.roll(x, half, axis=1) * sp
            + pltpu.roll(x, LANES - half, axis=1) * sm)


def _folded(c, n):
    return jnp.where(c < n // 2, c, n + n // 2 - 1 - c)


def _proj_kernel(x_ref, pos_ref, posprev_ref, tab_ref, ng_ref, w_ref, qg_ref, wuq_ref, kvg_ref,
                 wk_ref, wvt_ref, wdvt_ref,
                 dq_ref, dk_ref, dvt_ref, dg_ref, mq_ref, mk_ref, mvt_ref, mg_ref,
                 rowtab_ref, rope_ref, nb_ref, flag_ref):
    f32 = jnp.float32
    bf16 = jnp.bfloat16
    i = pl.program_id(0)
    row = lax.broadcasted_iota(jnp.int32, (BLK, LANES), 0).astype(f32)

    def store_tables(slot, t, c, s):
        rope_ref[slot, 3 * t] = c
        rope_ref[slot, 3 * t + 1] = s * tab_ref[3 * t + 1:3 * t + 2, :]
        rope_ref[slot, 3 * t + 2] = s * tab_ref[3 * t + 2:3 * t + 3, :]

    @pl.when(i == 0)
    def _():
        for t in range(2):
            ang = row * tab_ref[3 * t:3 * t + 1, :]
            rowtab_ref[2 * t] = jnp.cos(ang)
            rowtab_ref[2 * t + 1] = jnp.sin(ang)
        nb_ref[1] = jnp.zeros(nb_ref.shape[1:], bf16)
        rope_ref[1] = jnp.zeros(rope_ref.shape[1:], f32)
        flag_ref[1] = 0.0

    def prep(slot):
        nb_ref[slot] = _rms(x_ref[...], ng_ref[...]).astype(bf16)
        pos = pos_ref[...]
        base = pos[0:1, :]
        flag_ref[slot] = jnp.sum(jnp.where(pos == base + row, 0.0, 1.0))
        for t in range(2):
            ang = base * tab_ref[3 * t:3 * t + 1, :]
            cb, sb = jnp.cos(ang), jnp.sin(ang)
            cr, sr = rowtab_ref[2 * t], rowtab_ref[2 * t + 1]
            store_tables(slot, t, cb * cr - sb * sr, sb * cr + cb * sr)

    def direct_tables(slot):
        pos = posprev_ref[...]
        for t in range(2):
            ang = pos * tab_ref[3 * t:3 * t + 1, :]
            store_tables(slot, t, jnp.cos(ang), jnp.sin(ang))

    def main(slot):
        nb = nb_ref[slot]
        cd, spd, smd = rope_ref[slot, 0], rope_ref[slot, 1], rope_ref[slot, 2]
        cm, spm, smm = rope_ref[slot, 3], rope_ref[slot, 4], rope_ref[slot, 5]

        def proj(off, width):
            return jnp.dot(nb, w_ref[:, off:off + width], preferred_element_type=f32)

        q_scale = DIFF_HD ** -0.5 * LOG2E
        dq = proj(O_DQ, DIFF_WIDTH)
        dk = proj(O_DK, DIFF_WIDTH)
        for c in range(DIFF_WIDTH // LANES):
            sl = slice(c * LANES, (c + 1) * LANES)
            dq_ref[0, :, sl] = (_rope128(dq[:, sl], cd, spd, smd, DIFF_HD // 2)
                                * q_scale).astype(bf16)
            dk_ref[:, sl] = _rope128(dk[:, sl], cd, spd, smd, DIFF_HD // 2).astype(bf16)
        dvt_ref[0] = lax.dot_general(wdvt_ref[...], nb, _NT,
                                     preferred_element_type=f32).astype(bf16)
        dg_ref[0] = _silu(proj(O_DG, DIFF_WIDTH)).astype(bf16)
        mg_ref[0] = _silu(proj(O_MG, MLA_WIDTH)).astype(bf16)

        lat = proj(O_CQ, D_IN_P - O_CQ)
        cqn = _rms(lat[:, :MLA_Q_LORA], qg_ref[...]).astype(bf16)
        mq = jnp.dot(cqn, wuq_ref[...], preferred_element_type=f32)
        m_scale = (MLA_NOPE + MLA_ROPE) ** -0.5 * LOG2E

        ckvn = _rms(lat[:, O_CKV - O_CQ:O_KR - O_CQ], kvg_ref[...]).astype(bf16)
        kn = jnp.dot(ckvn, wk_ref[...], preferred_element_type=f32)
        mvt_ref[0] = lax.dot_general(wvt_ref[...], ckvn, _NT,
                                     preferred_element_type=f32).astype(bf16)
        kr = _rope128(lat[:, O_KR - O_CQ:], cm, spm, smm, MLA_ROPE // 2)
        for h in range(MLA_HEADS):
            sl = slice(h * LANES, (h + 1) * LANES)
            mq_ref[0, :, sl] = (_rope128(mq[:, sl], cm, spm, smm, MLA_ROPE // 2)
                                * m_scale).astype(bf16)
            mk_ref[:, sl] = (kn[:, sl] + kr).astype(bf16)

    def step(cur):
        prv = 1 - cur

        @pl.when(flag_ref[prv] != 0.0)
        def _():
            direct_tables(prv)

        main(prv)
        prep(cur)

    @pl.when(i % 2 == 0)
    def _():
        step(0)

    @pl.when(i % 2 == 1)
    def _():
        step(1)


def _proj_call(x2, posb, tab, ng, w_in_p, qg, wuq_p, kvg, wk_p, wvt_p, wdvt, nq):
    t = x2.shape[0]
    n = t // BLK
    bf16 = jnp.bfloat16
    nxt = lambda i: jnp.minimum(i, n - 1)
    cur = lambda i: jnp.maximum(i - 1, 0)
    row_in = lambda w, blk: pl.BlockSpec((BLK, w), lambda i: (blk(i), 0))
    full = lambda a: pl.BlockSpec(a.shape, lambda i: (0, 0))
    row = lambda w: pl.BlockSpec((BLK, w), lambda i: (cur(i), 0))
    blk3 = lambda w: pl.BlockSpec((1, w, BLK), lambda i: (cur(i), 0, 0))
    fold = lambda w: pl.BlockSpec(
        (1, BLK, w), lambda i: (cur(i) // nq * nq + _folded(cur(i) % nq, nq), 0, 0))
    rows = lambda w: jax.ShapeDtypeStruct((t, w), bf16)
    blks = lambda w: jax.ShapeDtypeStruct((n, w, BLK), bf16)
    folds = lambda w: jax.ShapeDtypeStruct((n, BLK, w), bf16)
    return pl.pallas_call(
        _proj_kernel,
        grid=(n + 1,),
        in_specs=[row_in(D_MODEL, nxt), row_in(LANES, nxt), row_in(LANES, cur), full(tab),
                  full(ng), full(w_in_p), full(qg), full(wuq_p), full(kvg), full(wk_p),
                  full(wvt_p), full(wdvt)],
        out_specs=[fold(DIFF_WIDTH), row(DIFF_WIDTH), blk3(DIFF_WIDTH), fold(DIFF_WIDTH),
                   fold(MLA_HEADS * LANES), row(MLA_HEADS * LANES), blk3(MLA_WIDTH),
                   fold(MLA_WIDTH)],
        out_shape=[folds(DIFF_WIDTH), rows(DIFF_WIDTH), blks(DIFF_WIDTH), folds(DIFF_WIDTH),
                   folds(MLA_HEADS * LANES), rows(MLA_HEADS * LANES), blks(MLA_WIDTH),
                   folds(MLA_WIDTH)],
        scratch_shapes=[pltpu.VMEM((4, BLK, LANES), jnp.float32),
                        pltpu.VMEM((2, 6, BLK, LANES), jnp.float32),
                        pltpu.VMEM((2, BLK, D_MODEL), bf16),
                        pltpu.SMEM((2,), jnp.float32)],
        compiler_params=pltpu.CompilerParams(
            dimension_semantics=("arbitrary",), vmem_limit_bytes=VMEM_LIMIT),
        name="proj",
    )(x2, posb, posb, tab, ng, w_in_p, qg, wuq_p, kvg, wk_p, wvt_p, wdvt)


SUM_LO, SUM_HI = 2.0 ** -64, 2.0 ** 64
OUT_LIMIT = 2.0 ** 100


def _attn_fast(nq, get_q, get_k, vt_ref, acc_ref, l_ref, epilogue):
    f32 = jnp.float32
    c = pl.program_id(2)
    acc_ref[...] = jnp.zeros(acc_ref.shape, f32)
    l_ref[...] = jnp.zeros(l_ref.shape, f32)

    def diagonal(side, j):
        return ([(side, j, i, 0, HALF, 0, BLK, True) for i in range(2)]
                + [(side, j, i, HALF, HALF, HALF, HALF, True) for i in range(2)])

    units = diagonal(0, c)
    for t in range(nq - 1):
        on_side0 = t < c
        side = jnp.where(on_side0, 0, 1)
        j = jnp.where(on_side0, t, t - c)
        units += [(side, j, i, 0, BLK, 0, BLK, False) for i in range(2)]
        if t == nq // 2 - 2:
            side0_done = len(units)
    units += diagonal(1, nq - 1 - c)
    bad0 = None

    def qk(u):
        side, j, i, k0, nk, q0, nq_, _ = units[u]
        return lax.dot_general(get_k(i, j, k0, nk), get_q(side, i, q0, nq_), _NT,
                               preferred_element_type=f32)

    scores = {u: qk(u) for u in range(2)}
    for u, (side, j, i, k0, nk, q0, nq_, masked) in enumerate(units):
        s = scores.pop(u)
        if masked:
            krow = lax.broadcasted_iota(jnp.int32, s.shape, 0) + k0
            qcol = lax.broadcasted_iota(jnp.int32, s.shape, 1) + q0
            s = jnp.where(krow <= qcol, s, -jnp.inf)
        p = jnp.exp2(s)
        l_ref[side, i, :, q0:q0 + nq_] += jnp.sum(p, axis=0, keepdims=True)
        pb = p.astype(jnp.bfloat16)
        if u + 2 < len(units):
            scores[u + 2] = qk(u + 2)
        acc_ref[side, i, :, q0:q0 + nq_] += jnp.dot(vt_ref[0, j, :, k0:k0 + nk], pb,
                                                    preferred_element_type=f32)
        if u + 1 == side0_done:
            bad0 = epilogue(0)
    return bad0 + epilogue(1)


def _attn_rescaling(nq, get_q, get_k, vt_ref, acc_ref, l_ref):
    f32 = jnp.float32
    c = pl.program_id(2)
    acc_ref[...] = jnp.zeros(acc_ref.shape, f32)

    for side, n_full in ((0, c), (1, nq - 1 - c)):
        def step(j, carry, masked, side=side):
            vt = vt_ref[0, j]
            out = []
            for i in range(2):
                m_prev, l_prev = carry[2 * i], carry[2 * i + 1]
                s = lax.dot_general(get_k(i, j, 0, BLK), get_q(side, i, 0, BLK), _NT,
                                    preferred_element_type=f32)
                if masked:
                    krow = lax.broadcasted_iota(jnp.int32, s.shape, 0)
                    qcol = lax.broadcasted_iota(jnp.int32, s.shape, 1)
                    s = jnp.where(krow <= qcol, s, -jnp.inf)
                m_new = jnp.maximum(m_prev, jnp.max(s, axis=0, keepdims=True))
                alpha = jnp.exp2(m_prev - m_new)
                p = jnp.exp2(s - m_new)
                l_new = alpha * l_prev + jnp.sum(p, axis=0, keepdims=True)
                acc_ref[side, i] = alpha * acc_ref[side, i] + jnp.dot(
                    vt, p.astype(jnp.bfloat16), preferred_element_type=f32)
                out += [m_new, l_new]
            return tuple(out)

        neg = jnp.full((1, BLK), -jnp.inf, f32)
        zero = jnp.zeros((1, BLK), f32)
        carry = lax.fori_loop(0, n_full, lambda j, cr: step(j, cr, False),
                              (neg, zero, neg, zero))
        carry = step(n_full, carry, True)
        l_ref[side, 0] = carry[1]
        l_ref[side, 1] = carry[3]


def _sums_out_of_range(l0, l1):
    ok = ((l0 > SUM_LO) & (l0 < SUM_HI)) & ((l1 > SUM_LO) & (l1 < SUM_HI))
    return jnp.where(ok, 0.0, 1.0)


def _guarded(nq, get_q, get_k, vt_ref, acc_ref, l_ref, epilogue):
    n_bad = jnp.sum(_attn_fast(nq, get_q, get_k, vt_ref, acc_ref, l_ref, epilogue))

    @pl.when(n_bad > 0.0)
    def _():
        _attn_rescaling(nq, get_q, get_k, vt_ref, acc_ref, l_ref)
        epilogue(0)
        epilogue(1)


def _key_rows(j, k0, n):
    return pl.ds(pl.multiple_of(j * BLK + k0, HALF), n)


def _diff_kernel(nq, lam_ref, q_ref, k_ref, vt_ref, g_ref, sg_ref, o_ref, qz_ref, acc_ref, l_ref):
    f32 = jnp.float32
    for side in range(2):
        q = q_ref[0, side, 0]
        lane = lax.broadcasted_iota(jnp.int32, q.shape, 1)
        zero = jnp.zeros_like(q)
        qz_ref[side, 0] = jnp.where(lane < DIFF_HD, q, zero)
        qz_ref[side, 1] = jnp.where(lane >= DIFF_HD, q, zero)

    lam_rows = lam_ref[...]
    lam_init = 0.8 - 0.6 * math.exp(-0.3 * 0)
    lam = (jnp.exp(jnp.sum(lam_rows[0:1] * lam_rows[1:2], axis=1, keepdims=True))
           - jnp.exp(jnp.sum(lam_rows[2:3] * lam_rows[3:4], axis=1, keepdims=True))
           + lam_init)

    def epilogue(side):
        l0, l1 = l_ref[side, 0], l_ref[side, 1]
        ot = acc_ref[side, 0] * (1.0 / l0) - acc_ref[side, 1] * (lam / l1)
        ms = jnp.mean(ot * ot, axis=0, keepdims=True)
        ot = ot * lax.rsqrt(ms + RMS_EPS)
        o = ot.T * (sg_ref[...] * (1.0 - lam_init)) * g_ref[0, side, 0].astype(f32)
        o_ref[0, side, 0] = o.astype(o_ref.dtype)
        return _sums_out_of_range(l0, l1) + jnp.where(ms < OUT_LIMIT, 0.0, 1.0)

    _guarded(nq,
             lambda side, i, q0, n: qz_ref[side, i, q0:q0 + n, :],
             lambda i, j, k0, n: k_ref[0, _key_rows(j, k0, n), :],
             vt_ref, acc_ref, l_ref, epilogue)


def _mla_kernel(nq, q_ref, k_ref, vt_ref, g_ref, o_ref, acc_ref, l_ref):
    f32 = jnp.float32

    def epilogue(side):
        l0, l1 = l_ref[side, 0], l_ref[side, 1]
        feat = lax.broadcasted_iota(jnp.int32, (LANES, BLK), 0)
        ot = jnp.where(feat < MLA_V, acc_ref[side, 0] * (1.0 / l0), acc_ref[side, 1] * (1.0 / l1))
        peak = jnp.max(jnp.abs(ot), axis=0, keepdims=True)
        o_ref[0, side, 0] = (ot.T * g_ref[0, side, 0].astype(f32)).astype(o_ref.dtype)
        return _sums_out_of_range(l0, l1) + jnp.where(peak < OUT_LIMIT, 0.0, 1.0)

    _guarded(nq,
             lambda side, i, q0, n: q_ref[0, side, 0, q0:q0 + n, i * LANES:(i + 1) * LANES],
             lambda i, j, k0, n: k_ref[0, _key_rows(j, k0, n), i * LANES:(i + 1) * LANES],
             vt_ref, acc_ref, l_ref, epilogue)


def _attn_scratch():
    return [pltpu.VMEM((2, 2, LANES, BLK), jnp.float32),
            pltpu.VMEM((2, 2, 1, BLK), jnp.float32)]


def _pair_spec(width):
    return pl.BlockSpec((1, 2, 1, BLK, width), lambda bi, h, c: (bi, 0, c, 0, h))


def _diff_call(lam_rows, dq, dk, dvt, dg, subg):
    b, _, npair, _, _ = dq.shape
    nq = 2 * npair
    s = nq * BLK
    kspec = pl.BlockSpec((1, s, LANES), lambda bi, h, c: (bi, 0, h))
    vtspec = pl.BlockSpec((1, nq, LANES, BLK), lambda bi, h, c: (bi, 0, h, 0))
    return pl.pallas_call(
        functools.partial(_diff_kernel, nq),
        grid=(b, DIFF_HEADS, npair),
        in_specs=[pl.BlockSpec(lam_rows.shape, lambda bi, h, c: (0, 0)),
                  _pair_spec(LANES), kspec, vtspec, _pair_spec(LANES),
                  pl.BlockSpec(subg.shape, lambda bi, h, c: (0, 0))],
        out_specs=_pair_spec(LANES),
        out_shape=jax.ShapeDtypeStruct(dq.shape, jnp.bfloat16),
        scratch_shapes=[pltpu.VMEM((2, 2, BLK, LANES), jnp.bfloat16)] + _attn_scratch(),
        compiler_params=pltpu.CompilerParams(
            dimension_semantics=("arbitrary",) * 3, vmem_limit_bytes=VMEM_LIMIT),
        name="diffattn",
    )(lam_rows, dq, dk, dvt, dg, subg)


def _mla_call(mq, mk, mvt, mg):
    b, _, npair, _, _ = mq.shape
    nq = 2 * npair
    s = nq * BLK
    return pl.pallas_call(
        functools.partial(_mla_kernel, nq),
        grid=(b, MLA_HEADS // 2, npair),
        in_specs=[_pair_spec(2 * LANES),
                  pl.BlockSpec((1, s, 2 * LANES), lambda bi, h, c: (bi, 0, h)),
                  pl.BlockSpec((1, nq, LANES, BLK), lambda bi, h, c: (bi, 0, h, 0)),
                  _pair_spec(LANES)],
        out_specs=_pair_spec(LANES),
        out_shape=jax.ShapeDtypeStruct(mg.shape, jnp.bfloat16),
        scratch_shapes=_attn_scratch(),
        compiler_params=pltpu.CompilerParams(
            dimension_semantics=("arbitrary",) * 3, vmem_limit_bytes=VMEM_LIMIT),
        name="mlaattn",
    )(mq, mk, mvt, mg)


def _out_kernel(x_ref, od_ref, om_ref, p_ref, wo_ref, wp_ref, wg_ref, fg_ref, o_ref):
    f32 = jnp.float32
    bf16 = jnp.bfloat16
    h = (x_ref[...]
         + jnp.dot(od_ref[0], wo_ref[0:DIFF_WIDTH, :], preferred_element_type=f32)
         + jnp.dot(om_ref[0], wo_ref[DIFF_WIDTH:, :], preferred_element_type=f32))
    gate = jax.nn.sigmoid(jnp.dot(h.astype(bf16), wg_ref[...], preferred_element_type=f32))
    emb = jnp.dot(p_ref[...].astype(bf16), wp_ref[...], preferred_element_type=f32)
    h = h + emb * gate
    o_ref[...] = _rms(h, fg_ref[...])


def _out_call(x2, od, om, p2, wo, wp, wg, fg, nq):
    t = x2.shape[0]
    row = lambda w: pl.BlockSpec((BLK, w), lambda i: (i, 0))
    full = lambda a: pl.BlockSpec(a.shape, lambda i: (0, 0))
    fold = lambda w: pl.BlockSpec((1, BLK, w), lambda i: (i // nq * nq + _folded(i % nq, nq), 0, 0))
    return pl.pallas_call(
        _out_kernel,
        grid=(t // BLK,),
        in_specs=[row(D_MODEL), fold(DIFF_WIDTH), fold(MLA_WIDTH), row(PLE_DIM),
                  full(wo), full(wp), full(wg), full(fg)],
        out_specs=row(D_MODEL),
        out_shape=jax.ShapeDtypeStruct((t, D_MODEL), jnp.float32),
        compiler_params=pltpu.CompilerParams(
            dimension_semantics=("arbitrary",), vmem_limit_bytes=VMEM_LIMIT),
        name="outproj",
    )(x2, od, om, p2, wo, wp, wg, fg)


def _rope_tables():
    f32 = jnp.float32
    lane = jnp.arange(LANES)
    inv_d = ROPE_THETA ** (-jnp.arange(0, DIFF_HD, 2, dtype=f32) / DIFF_HD)
    half_d = DIFF_HD // 2
    inv_d_l = inv_d[lane % half_d]
    sp_d = jnp.where(lane % DIFF_HD >= half_d, 1.0, 0.0)
    sm_d = jnp.where(lane % DIFF_HD < half_d, -1.0, 0.0)
    inv_m = ROPE_THETA ** (-jnp.arange(0, MLA_ROPE, 2, dtype=f32) / MLA_ROPE)
    half_m = MLA_ROPE // 2
    in_rope = (lane >= MLA_NOPE) & (lane < MLA_NOPE + MLA_ROPE)
    inv_m_l = jnp.where(in_rope, inv_m[(lane - MLA_NOPE) % half_m], 0.0)
    sp_m = jnp.where(in_rope & (lane >= MLA_NOPE + half_m), 1.0, 0.0)
    sm_m = jnp.where(in_rope & (lane < MLA_NOPE + half_m), -1.0, 0.0)
    zero = jnp.zeros((LANES,), f32)
    return jnp.stack([inv_d_l, sp_d, sm_d, inv_m_l, sp_m, sm_m, zero, zero]).astype(f32)


def kernel(x, p, positions, norm_g, w_in, diff_lambda, diff_subln_g, mla_q_norm_g, w_uq,
           mla_kv_norm_g, w_ukv, w_out, w_ple, w_ple_gate, final_norm_g):
    b, s, d = x.shape
    t = b * s
    nq = s // BLK
    assert s % (2 * BLK) == 0
    f32 = jnp.float32
    bf16 = jnp.bfloat16
    i = 0

    offs = [0, 512, 1024, 1536, 2048, 2432, 2560, 2592, 3104]
    seg = [w_in[i][:, offs[j]:offs[j + 1]] for j in range(8)]
    wdq, wdk, wdv, wdg, wcq, wckv, wkr, wmg = seg
    zc = lambda n: jnp.zeros((d, n), f32)
    w_in_p = jnp.concatenate(
        [wdq, wdk, wdg, wmg, wcq, wckv, zc(MLA_NOPE), wkr, zc(LANES - MLA_NOPE - MLA_ROPE)],
        axis=1).astype(bf16)
    wuq_p = jnp.pad(w_uq[i].reshape(MLA_Q_LORA, MLA_HEADS, MLA_NOPE + MLA_ROPE),
                    ((0, 0), (0, 0), (0, LANES - MLA_NOPE - MLA_ROPE))
                    ).reshape(MLA_Q_LORA, MLA_HEADS * LANES).astype(bf16)
    wkv3 = w_ukv[i].reshape(MLA_KV_LORA, MLA_HEADS, MLA_NOPE + MLA_V)
    wk_p = jnp.pad(wkv3[:, :, :MLA_NOPE], ((0, 0), (0, 0), (0, LANES - MLA_NOPE))
                   ).reshape(MLA_KV_LORA, MLA_HEADS * LANES).astype(bf16)
    wvt_p = wkv3[:, :, MLA_NOPE:].reshape(MLA_KV_LORA, MLA_WIDTH).T.astype(bf16)
    wdvt = wdv.T.astype(bf16)

    x2 = x.reshape(t, d)
    posb = jnp.broadcast_to(positions.astype(f32).reshape(t, 1), (t, LANES))
    dq, dk, dvt, dg, mq, mk, mvt, mg = _proj_call(
        x2, posb, _rope_tables(), norm_g[i].reshape(1, d), w_in_p,
        mla_q_norm_g[i].reshape(1, MLA_Q_LORA), wuq_p,
        mla_kv_norm_g[i].reshape(1, MLA_KV_LORA), wk_p, wvt_p, wdvt, nq)

    r3 = lambda a: a.reshape(b, s, a.shape[-1])
    r4 = lambda a: a.reshape(b, nq, a.shape[-2], BLK)
    r5 = lambda a: a.reshape(b, 2, nq // 2, BLK, a.shape[-1])
    od = _diff_call(diff_lambda[i].astype(f32), r5(dq), r3(dk), r4(dvt), r5(dg),
                    diff_subln_g[i].reshape(1, 2 * DIFF_HD))
    om = _mla_call(r5(mq), r3(mk), r4(mvt), r5(mg))

    out = _out_call(x2, od.reshape(t // BLK, BLK, DIFF_WIDTH), om.reshape(t // BLK, BLK, MLA_WIDTH),
                    p[i].reshape(t, PLE_DIM), w_out[i].astype(bf16), w_ple[i].astype(bf16),
                    w_ple_gate[i].astype(bf16), final_norm_g.reshape(1, d), nq)
    return out.reshape(b, s, d)
```

```python
import functools
import math

import jax
import jax.numpy as jnp
from jax import lax
from jax.experimental import pallas as pl
from jax.experimental.pallas import tpu as pltpu

D_MODEL = 1024
PLE_DIM = 256
ROPE_THETA = 10000.0
RMS_EPS = 1e-6

DIFF_WIDTH = 512
DIFF_HEADS = 4
DIFF_HD = 64
MLA_WIDTH = 512
MLA_HEADS = 8
MLA_NOPE = 64
MLA_ROPE = 32
MLA_V = 64
MLA_Q_LORA = 384
MLA_KV_LORA = 128

LANES = 128
LOG2E = math.log2(math.e)
VMEM_LIMIT = 48 * 1024 * 1024

O_DQ, O_DK, O_DG, O_MG = 0, 512, 1024, 1536
O_CQ = 2048
O_CKV = O_CQ + MLA_Q_LORA
O_KR = O_CKV + MLA_KV_LORA
D_IN_P = O_KR + LANES

_NT = (((1,), (1,)), ((), ()))
BLK = 512
HALF = BLK // 2
PAIRS = 2


def _rms(x, g):
    return x * lax.rsqrt(jnp.mean(x * x, axis=-1, keepdims=True) + RMS_EPS) * g


def _silu(x):
    return x / (1.0 + jnp.exp(-x))


def _rope128(x, c, sp, sm, half):
    return (x * c + pltpu.roll(x, half, axis=1) * sp
            + pltpu.roll(x, LANES - half, axis=1) * sm)


def _folded(c, n):
    return jnp.where(c < n // 2, c, n + n // 2 - 1 - c)


def _proj_kernel(x_ref, pos_ref, posprev_ref, tab_ref, ng_ref, w_ref, qg_ref, wuq_ref, kvg_ref,
                 wk_ref, wvt_ref, wdvt_ref,
                 dq_ref, dk_ref, dvt_ref, dg_ref, mq_ref, mk_ref, mvt_ref, mg_ref,
                 rowtab_ref, rope_ref, nb_ref, flag_ref):
    f32 = jnp.float32
    bf16 = jnp.bfloat16
    i = pl.program_id(0)
    row = lax.broadcasted_iota(jnp.int32, (BLK, LANES), 0).astype(f32)

    def store_tables(slot, t, c, s):
        rope_ref[slot, 3 * t] = c
        rope_ref[slot, 3 * t + 1] = s * tab_ref[3 * t + 1:3 * t + 2, :]
        rope_ref[slot, 3 * t + 2] = s * tab_ref[3 * t + 2:3 * t + 3, :]

    @pl.when(i == 0)
    def _():
        for t in range(2):
            ang = row * tab_ref[3 * t:3 * t + 1, :]
            rowtab_ref[2 * t] = jnp.cos(ang)
            rowtab_ref[2 * t + 1] = jnp.sin(ang)
        nb_ref[1] = jnp.zeros(nb_ref.shape[1:], bf16)
        rope_ref[1] = jnp.zeros(rope_ref.shape[1:], f32)
        flag_ref[1] = 0.0

    def prep(slot):
        nb_ref[slot] = _rms(x_ref[...], ng_ref[...]).astype(bf16)
        pos = pos_ref[...]
        base = pos[0:1, :]
        flag_ref[slot] = jnp.sum(jnp.where(pos == base + row, 0.0, 1.0))
        for t in range(2):
            ang = base * tab_ref[3 * t:3 * t + 1, :]
            cb, sb = jnp.cos(ang), jnp.sin(ang)
            cr, sr = rowtab_ref[2 * t], rowtab_ref[2 * t + 1]
            store_tables(slot, t, cb * cr - sb * sr, sb * cr + cb * sr)

    def direct_tables(slot):
        pos = posprev_ref[...]
        for t in range(2):
            ang = pos * tab_ref[3 * t:3 * t + 1, :]
            store_tables(slot, t, jnp.cos(ang), jnp.sin(ang))

    def main(slot):
        nb = nb_ref[slot]
        cd, spd, smd = rope_ref[slot, 0], rope_ref[slot, 1], rope_ref[slot, 2]
        cm, spm, smm = rope_ref[slot, 3], rope_ref[slot, 4], rope_ref[slot, 5]

        def proj(off, width):
            return jnp.dot(nb, w_ref[:, off:off + width], preferred_element_type=f32)

        q_scale = DIFF_HD ** -0.5 * LOG2E
        dq = proj(O_DQ, DIFF_WIDTH)
        dk = proj(O_DK, DIFF_WIDTH)
        for c in range(DIFF_WIDTH // LANES):
            sl = slice(c * LANES, (c + 1) * LANES)
            dq_ref[0, :, sl] = (_rope128(dq[:, sl], cd, spd, smd, DIFF_HD // 2)
                                * q_scale).astype(bf16)
            dk_ref[:, sl] = _rope128(dk[:, sl], cd, spd, smd, DIFF_HD // 2).astype(bf16)
        dvt_ref[0] = lax.dot_general(wdvt_ref[...], nb, _NT,
                                     preferred_element_type=f32).astype(bf16)
        dg_ref[0] = _silu(proj(O_DG, DIFF_WIDTH)).astype(bf16)
        mg_ref[0] = _silu(proj(O_MG, MLA_WIDTH)).astype(bf16)

        lat = proj(O_CQ, D_IN_P - O_CQ)
        cqn = _rms(lat[:, :MLA_Q_LORA], qg_ref[...]).astype(bf16)
        mq = jnp.dot(cqn, wuq_ref[...], preferred_element_type=f32)
        m_scale = (MLA_NOPE + MLA_ROPE) ** -0.5 * LOG2E

        ckvn = _rms(lat[:, O_CKV - O_CQ:O_KR - O_CQ], kvg_ref[...]).astype(bf16)
        kn = jnp.dot(ckvn, wk_ref[...], preferred_element_type=f32)
        mvt_ref[0] = lax.dot_general(wvt_ref[...], ckvn, _NT,
                                     preferred_element_type=f32).astype(bf16)
        kr = _rope128(lat[:, O_KR - O_CQ:], cm, spm, smm, MLA_ROPE // 2)
        for h in range(MLA_HEADS):
            sl = slice(h * LANES, (h + 1) * LANES)
            mq_ref[0, :, sl] = (_rope128(mq[:, sl], cm, spm, smm, MLA_ROPE // 2)
                                * m_scale).astype(bf16)
            mk_ref[:, sl] = (kn[:, sl] + kr).astype(bf16)

    def step(cur):
        prv = 1 - cur

        @pl.when(flag_ref[prv] != 0.0)
        def _():
            direct_tables(prv)

        main(prv)
        prep(cur)

    @pl.when(i % 2 == 0)
    def _():
        step(0)

    @pl.when(i % 2 == 1)
    def _():
        step(1)


def _proj_call(x2, posb, tab, ng, w_in_p, qg, wuq_p, kvg, wk_p, wvt_p, wdvt, nq):
    t = x2.shape[0]
    n = t // BLK
    bf16 = jnp.bfloat16
    nxt = lambda i: jnp.minimum(i, n - 1)
    cur = lambda i: jnp.maximum(i - 1, 0)
    row_in = lambda w, blk: pl.BlockSpec((BLK, w), lambda i: (blk(i), 0))
    full = lambda a: pl.BlockSpec(a.shape, lambda i: (0, 0))
    row = lambda w: pl.BlockSpec((BLK, w), lambda i: (cur(i), 0))
    blk3 = lambda w: pl.BlockSpec((1, w, BLK), lambda i: (cur(i), 0, 0))
    fold = lambda w: pl.BlockSpec(
        (1, BLK, w), lambda i: (cur(i) // nq * nq + _folded(cur(i) % nq, nq), 0, 0))
    rows = lambda w: jax.ShapeDtypeStruct((t, w), bf16)
    blks = lambda w: jax.ShapeDtypeStruct((n, w, BLK), bf16)
    folds = lambda w: jax.ShapeDtypeStruct((n, BLK, w), bf16)
    return pl.pallas_call(
        _proj_kernel,
        grid=(n + 1,),
        in_specs=[row_in(D_MODEL, nxt), row_in(LANES, nxt), row_in(LANES, cur), full(tab),
                  full(ng), full(w_in_p), full(qg), full(wuq_p), full(kvg), full(wk_p),
                  full(wvt_p), full(wdvt)],
        out_specs=[fold(DIFF_WIDTH), row(DIFF_WIDTH), blk3(DIFF_WIDTH), fold(DIFF_WIDTH),
                   fold(MLA_HEADS * LANES), row(MLA_HEADS * LANES), blk3(MLA_WIDTH),
                   fold(MLA_WIDTH)],
        out_shape=[folds(DIFF_WIDTH), rows(DIFF_WIDTH), blks(DIFF_WIDTH), folds(DIFF_WIDTH),
                   folds(MLA_HEADS * LANES), rows(MLA_HEADS * LANES), blks(MLA_WIDTH),
                   folds(MLA_WIDTH)],
        scratch_shapes=[pltpu.VMEM((4, BLK, LANES), jnp.float32),
                        pltpu.VMEM((2, 6, BLK, LANES), jnp.float32),
                        pltpu.VMEM((2, BLK, D_MODEL), bf16),
                        pltpu.SMEM((2,), jnp.float32)],
        compiler_params=pltpu.CompilerParams(
            dimension_semantics=("arbitrary",), vmem_limit_bytes=VMEM_LIMIT),
        name="proj",
    )(x2, posb, posb, tab, ng, w_in_p, qg, wuq_p, kvg, wk_p, wvt_p, wdvt)


SUM_LO, SUM_HI = 2.0 ** -64, 2.0 ** 64
OUT_LIMIT = 2.0 ** 100


def _attn_fast(nq, get_q, get_k, vt_ref, acc_ref, l_ref, epilogue):
    f32 = jnp.float32
    first = pl.program_id(2) * PAIRS

    def diagonal(side, j):
        return ([(side, j, i, 0, HALF, 0, BLK, True, True) for i in range(2)]
                + [(side, j, i, HALF, HALF, HALF, HALF, True, False) for i in range(2)])

    units, done = [], {}
    for p in range(PAIRS):
        c = first + p
        units += diagonal((p, 0), c) + diagonal((p, 1), nq - 1 - c)
        for t in range(nq - 1):
            on_side0 = t < c
            side = (p, jnp.where(on_side0, 0, 1) if t < nq // 2 - 1 else 1)
            j = jnp.where(on_side0, t, t - c)
            units += [(side, j, i, 0, BLK, 0, BLK, False, False) for i in range(2)]
            if t == nq // 2 - 2:
                done[len(units)] = (p, 0)
        done[len(units)] = (p, 1)

    def qk(u):
        side, j, i, k0, nk, q0, nq_ = units[u][:7]
        return lax.dot_general(get_k(i, j, k0, nk), get_q(side, i, q0, nq_), _NT,
                               preferred_element_type=f32)

    bad = jnp.zeros((1, BLK), f32)
    scores = {u: qk(u) for u in range(2)}
    for u, (side, j, i, k0, nk, q0, nq_, masked, init) in enumerate(units):
        s = scores.pop(u)
        if masked:
            krow = lax.broadcasted_iota(jnp.int32, s.shape, 0) + k0
            qcol = lax.broadcasted_iota(jnp.int32, s.shape, 1) + q0
            s = jnp.where(krow <= qcol, s, -jnp.inf)
        p = jnp.exp2(s)
        col = jnp.sum(p, axis=0, keepdims=True)
        pb = p.astype(jnp.bfloat16)
        if u + 2 < len(units):
            scores[u + 2] = qk(u + 2)
        pv = jnp.dot(vt_ref[0, j, :, k0:k0 + nk], pb, preferred_element_type=f32)
        where = side + (i, slice(None), slice(q0, q0 + nq_))
        if init:
            l_ref[where] = col
            acc_ref[where] = pv
        else:
            l_ref[where] += col
            acc_ref[where] += pv
        if u + 1 in done:
            bad = bad + epilogue(done[u + 1])
    return bad


def _sides():
    return [(p, s) for p in range(PAIRS) for s in range(2)]


def _attn_rescaling(nq, get_q, get_k, vt_ref, acc_ref, l_ref):
    f32 = jnp.float32
    first = pl.program_id(2) * PAIRS
    acc_ref[...] = jnp.zeros(acc_ref.shape, f32)

    for side in _sides():
        n_full = first + side[0] if side[1] == 0 else nq - 1 - first - side[0]

        def step(j, carry, masked, side=side):
            vt = vt_ref[0, j]
            out = []
            for i in range(2):
                m_prev, l_prev = carry[2 * i], carry[2 * i + 1]
                s = lax.dot_general(get_k(i, j, 0, BLK), get_q(side, i, 0, BLK), _NT,
                                    preferred_element_type=f32)
                if masked:
                    krow = lax.broadcasted_iota(jnp.int32, s.shape, 0)
                    qcol = lax.broadcasted_iota(jnp.int32, s.shape, 1)
                    s = jnp.where(krow <= qcol, s, -jnp.inf)
                m_new = jnp.maximum(m_prev, jnp.max(s, axis=0, keepdims=True))
                alpha = jnp.exp2(m_prev - m_new)
                p = jnp.exp2(s - m_new)
                l_new = alpha * l_prev + jnp.sum(p, axis=0, keepdims=True)
                acc_ref[side + (i,)] = alpha * acc_ref[side + (i,)] + jnp.dot(
                    vt, p.astype(jnp.bfloat16), preferred_element_type=f32)
                out += [m_new, l_new]
            return tuple(out)

        neg = jnp.full((1, BLK), -jnp.inf, f32)
        zero = jnp.zeros((1, BLK), f32)
        carry = lax.fori_loop(0, n_full, lambda j, cr: step(j, cr, False),
                              (neg, zero, neg, zero))
        carry = step(n_full, carry, True)
        l_ref[side + (0,)] = carry[1]
        l_ref[side + (1,)] = carry[3]


def _sums_out_of_range(l0, l1):
    ok = ((l0 > SUM_LO) & (l0 < SUM_HI)) & ((l1 > SUM_LO) & (l1 < SUM_HI))
    return jnp.where(ok, 0.0, 1.0)


def _guarded(nq, get_q, get_k, vt_ref, acc_ref, l_ref, epilogue):
    n_bad = jnp.sum(_attn_fast(nq, get_q, get_k, vt_ref, acc_ref, l_ref, epilogue))

    @pl.when(n_bad > 0.0)
    def _():
        _attn_rescaling(nq, get_q, get_k, vt_ref, acc_ref, l_ref)
        for side in _sides():
            epilogue(side)


def _key_rows(j, k0, n):
    return pl.ds(pl.multiple_of(j * BLK + k0, HALF), n)


def _diff_kernel(nq, lam_ref, q_ref, k_ref, vt_ref, g_ref, sg_ref, o_ref, qz_ref, acc_ref, l_ref):
    f32 = jnp.float32
    for p, s in _sides():
        q = q_ref[0, s, p]
        lane = lax.broadcasted_iota(jnp.int32, q.shape, 1)
        zero = jnp.zeros_like(q)
        qz_ref[p, s, 0] = jnp.where(lane < DIFF_HD, q, zero)
        qz_ref[p, s, 1] = jnp.where(lane >= DIFF_HD, q, zero)

    lam_rows = lam_ref[...]
    lam_init = 0.8 - 0.6 * math.exp(-0.3 * 0)
    lam = (jnp.exp(jnp.sum(lam_rows[0:1] * lam_rows[1:2], axis=1, keepdims=True))
           - jnp.exp(jnp.sum(lam_rows[2:3] * lam_rows[3:4], axis=1, keepdims=True))
           + lam_init)

    def epilogue(side):
        p, s = side
        l0, l1 = l_ref[p, s, 0], l_ref[p, s, 1]
        ot = acc_ref[p, s, 0] * (1.0 / l0) - acc_ref[p, s, 1] * (lam / l1)
        ms = jnp.mean(ot * ot, axis=0, keepdims=True)
        ot = ot * lax.rsqrt(ms + RMS_EPS)
        o = ot.T * (sg_ref[...] * (1.0 - lam_init)) * g_ref[0, s, p].astype(f32)
        o_ref[0, s, p] = o.astype(o_ref.dtype)
        return _sums_out_of_range(l0, l1) + jnp.where(ms < OUT_LIMIT, 0.0, 1.0)

    _guarded(nq,
             lambda side, i, q0, n: qz_ref[side[0], side[1], i, q0:q0 + n, :],
             lambda i, j, k0, n: k_ref[0, _key_rows(j, k0, n), :],
             vt_ref, acc_ref, l_ref, epilogue)


def _mla_kernel(nq, q_ref, k_ref, vt_ref, g_ref, o_ref, acc_ref, l_ref):
    f32 = jnp.float32

    def epilogue(side):
        p, s = side
        l0, l1 = l_ref[p, s, 0], l_ref[p, s, 1]
        feat = lax.broadcasted_iota(jnp.int32, (LANES, BLK), 0)
        ot = jnp.where(feat < MLA_V, acc_ref[p, s, 0] * (1.0 / l0), acc_ref[p, s, 1] * (1.0 / l1))
        peak = jnp.max(jnp.abs(ot), axis=0, keepdims=True)
        o_ref[0, s, p] = (ot.T * g_ref[0, s, p].astype(f32)).astype(o_ref.dtype)
        return _sums_out_of_range(l0, l1) + jnp.where(peak < OUT_LIMIT, 0.0, 1.0)

    _guarded(nq,
             lambda side, i, q0, n: q_ref[0, side[1], side[0], q0:q0 + n,
                                          i * LANES:(i + 1) * LANES],
             lambda i, j, k0, n: k_ref[0, _key_rows(j, k0, n), i * LANES:(i + 1) * LANES],
             vt_ref, acc_ref, l_ref, epilogue)


def _attn_scratch():
    return [pltpu.VMEM((PAIRS, 2, 2, LANES, BLK), jnp.float32),
            pltpu.VMEM((PAIRS, 2, 2, 1, BLK), jnp.float32)]


def _pair_spec(width):
    return pl.BlockSpec((1, 2, PAIRS, BLK, width), lambda bi, h, c: (bi, 0, c, 0, h))


def _diff_call(lam_rows, dq, dk, dvt, dg, subg):
    b, _, npair, _, _ = dq.shape
    nq = 2 * npair
    s = nq * BLK
    kspec = pl.BlockSpec((1, s, LANES), lambda bi, h, c: (bi, 0, h))
    vtspec = pl.BlockSpec((1, nq, LANES, BLK), lambda bi, h, c: (bi, 0, h, 0))
    return pl.pallas_call(
        functools.partial(_diff_kernel, nq),
        grid=(b, DIFF_HEADS, npair // PAIRS),
        in_specs=[pl.BlockSpec(lam_rows.shape, lambda bi, h, c: (0, 0)),
                  _pair_spec(LANES), kspec, vtspec, _pair_spec(LANES),
                  pl.BlockSpec(subg.shape, lambda bi, h, c: (0, 0))],
        out_specs=_pair_spec(LANES),
        out_shape=jax.ShapeDtypeStruct(dq.shape, jnp.bfloat16),
        scratch_shapes=[pltpu.VMEM((PAIRS, 2, 2, BLK, LANES), jnp.bfloat16)] + _attn_scratch(),
        compiler_params=pltpu.CompilerParams(
            dimension_semantics=("arbitrary",) * 3, vmem_limit_bytes=VMEM_LIMIT),
        name="diffattn",
    )(lam_rows, dq, dk, dvt, dg, subg)


def _mla_call(mq, mk, mvt, mg):
    b, _, npair, _, _ = mq.shape
    nq = 2 * npair
    s = nq * BLK
    return pl.pallas_call(
        functools.partial(_mla_kernel, nq),
        grid=(b, MLA_HEADS // 2, npair // PAIRS),
        in_specs=[_pair_spec(2 * LANES),
                  pl.BlockSpec((1, s, 2 * LANES), lambda bi, h, c: (bi, 0, h)),
                  pl.BlockSpec((1, nq, LANES, BLK), lambda bi, h, c: (bi, 0, h, 0)),
                  _pair_spec(LANES)],
        out_specs=_pair_spec(LANES),
        out_shape=jax.ShapeDtypeStruct(mg.shape, jnp.bfloat16),
        scratch_shapes=_attn_scratch(),
        compiler_params=pltpu.CompilerParams(
            dimension_semantics=("arbitrary",) * 3, vmem_limit_bytes=VMEM_LIMIT),
        name="mlaattn",
    )(mq, mk, mvt, mg)


def _out_kernel(x_ref, od_ref, om_ref, p_ref, wo_ref, wp_ref, wg_ref, fg_ref, o_ref):
    f32 = jnp.float32
    bf16 = jnp.bfloat16
    h = (x_ref[...]
         + jnp.dot(od_ref[0], wo_ref[0:DIFF_WIDTH, :], preferred_element_type=f32)
         + jnp.dot(om_ref[0], wo_ref[DIFF_WIDTH:, :], preferred_element_type=f32))
    gate = jax.nn.sigmoid(jnp.dot(h.astype(bf16), wg_ref[...], preferred_element_type=f32))
    emb = jnp.dot(p_ref[...].astype(bf16), wp_ref[...], preferred_element_type=f32)
    h = h + emb * gate
    o_ref[...] = _rms(h, fg_ref[...])


def _out_call(x2, od, om, p2, wo, wp, wg, fg, nq):
    t = x2.shape[0]
    row = lambda w: pl.BlockSpec((BLK, w), lambda i: (i, 0))
    full = lambda a: pl.BlockSpec(a.shape, lambda i: (0, 0))
    fold = lambda w: pl.BlockSpec((1, BLK, w), lambda i: (i // nq * nq + _folded(i % nq, nq), 0, 0))
    return pl.pallas_call(
        _out_kernel,
        grid=(t // BLK,),
        in_specs=[row(D_MODEL), fold(DIFF_WIDTH), fold(MLA_WIDTH), row(PLE_DIM),
                  full(wo), full(wp), full(wg), full(fg)],
        out_specs=row(D_MODEL),
        out_shape=jax.ShapeDtypeStruct((t, D_MODEL), jnp.float32),
        compiler_params=pltpu.CompilerParams(
            dimension_semantics=("arbitrary",), vmem_limit_bytes=VMEM_LIMIT),
        name="outproj",
    )(x2, od, om, p2, wo, wp, wg, fg)


def _rope_tables():
    f32 = jnp.float32
    lane = jnp.arange(LANES)
    inv_d = ROPE_THETA ** (-jnp.arange(0, DIFF_HD, 2, dtype=f32) / DIFF_HD)
    half_d = DIFF_HD // 2
    inv_d_l = inv_d[lane % half_d]
    sp_d = jnp.where(lane % DIFF_HD >= half_d, 1.0, 0.0)
    sm_d = jnp.where(lane % DIFF_HD < half_d, -1.0, 0.0)
    inv_m = ROPE_THETA ** (-jnp.arange(0, MLA_ROPE, 2, dtype=f32) / MLA_ROPE)
    half_m = MLA_ROPE // 2
    in_rope = (lane >= MLA_NOPE) & (lane < MLA_NOPE + MLA_ROPE)
    inv_m_l = jnp.where(in_rope, inv_m[(lane - MLA_NOPE) % half_m], 0.0)
    sp_m = jnp.where(in_rope & (lane >= MLA_NOPE + half_m), 1.0, 0.0)
    sm_m = jnp.where(in_rope & (lane < MLA_NOPE + half_m), -1.0, 0.0)
    zero = jnp.zeros((LANES,), f32)
    return jnp.stack([inv_d_l, sp_d, sm_d, inv_m_l, sp_m, sm_m, zero, zero]).astype(f32)


def kernel(x, p, positions, norm_g, w_in, diff_lambda, diff_subln_g, mla_q_norm_g, w_uq,
           mla_kv_norm_g, w_ukv, w_out, w_ple, w_ple_gate, final_norm_g):
    b, s, d = x.shape
    t = b * s
    nq = s // BLK
    assert s % (2 * PAIRS * BLK) == 0
    f32 = jnp.float32
    bf16 = jnp.bfloat16
    i = 0

    offs = [0, 512, 1024, 1536, 2048, 2432, 2560, 2592, 3104]
    seg = [w_in[i][:, offs[j]:offs[j + 1]] for j in range(8)]
    wdq, wdk, wdv, wdg, wcq, wckv, wkr, wmg = seg
    zc = lambda n: jnp.zeros((d, n), f32)
    w_in_p = jnp.concatenate(
        [wdq, wdk, wdg, wmg, wcq, wckv, zc(MLA_NOPE), wkr, zc(LANES - MLA_NOPE - MLA_ROPE)],
        axis=1).astype(bf16)
    wuq_p = jnp.pad(w_uq[i].reshape(MLA_Q_LORA, MLA_HEADS, MLA_NOPE + MLA_ROPE),
                    ((0, 0), (0, 0), (0, LANES - MLA_NOPE - MLA_ROPE))
                    ).reshape(MLA_Q_LORA, MLA_HEADS * LANES).astype(bf16)
    wkv3 = w_ukv[i].reshape(MLA_KV_LORA, MLA_HEADS, MLA_NOPE + MLA_V)
    wk_p = jnp.pad(wkv3[:, :, :MLA_NOPE], ((0, 0), (0, 0), (0, LANES - MLA_NOPE))
                   ).reshape(MLA_KV_LORA, MLA_HEADS * LANES).astype(bf16)
    wvt_p = wkv3[:, :, MLA_NOPE:].reshape(MLA_KV_LORA, MLA_WIDTH).T.astype(bf16)
    wdvt = wdv.T.astype(bf16)

    x2 = x.reshape(t, d)
    posb = jnp.broadcast_to(positions.astype(f32).reshape(t, 1), (t, LANES))
    dq, dk, dvt, dg, mq, mk, mvt, mg = _proj_call(
        x2, posb, _rope_tables(), norm_g[i].reshape(1, d), w_in_p,
        mla_q_norm_g[i].reshape(1, MLA_Q_LORA), wuq_p,
        mla_kv_norm_g[i].reshape(1, MLA_KV_LORA), wk_p, wvt_p, wdvt, nq)

    r3 = lambda a: a.reshape(b, s, a.shape[-1])
    r4 = lambda a: a.reshape(b, nq, a.shape[-2], BLK)
    r5 = lambda a: a.reshape(b, 2, nq // 2, BLK, a.shape[-1])
    od = _diff_call(diff_lambda[i].astype(f32), r5(dq), r3(dk), r4(dvt), r5(dg),
                    diff_subln_g[i].reshape(1, 2 * DIFF_HD))
    om = _mla_call(r5(mq), r3(mk), r4(mvt), r5(mg))

    out = _out_call(x2, od.reshape(t // BLK, BLK, DIFF_WIDTH), om.reshape(t // BLK, BLK, MLA_WIDTH),
                    p[i].reshape(t, PLE_DIM), w_out[i].astype(bf16), w_ple[i].astype(bf16),
                    w_ple_gate[i].astype(bf16), final_norm_g.reshape(1, d), nq)
    return out.reshape(b, s, d)
```

```python
import functools
import math

import jax
import jax.numpy as jnp
from jax import lax
from jax.experimental import pallas as pl
from jax.experimental.pallas import tpu as pltpu

D_MODEL = 1024
PLE_DIM = 256
ROPE_THETA = 10000.0
RMS_EPS = 1e-6

DIFF_WIDTH = 512
DIFF_HEADS = 4
DIFF_HD = 64
MLA_WIDTH = 512
MLA_HEADS = 8
MLA_NOPE = 64
MLA_ROPE = 32
MLA_V = 64
MLA_Q_LORA = 384
MLA_KV_LORA = 128

LANES = 128
LOG2E = math.log2(math.e)
VMEM_LIMIT = 48 * 1024 * 1024

O_DQ, O_DK, O_DG, O_MG = 0, 512, 1024, 1536
O_CQ = 2048
O_CKV = O_CQ + MLA_Q_LORA
O_KR = O_CKV + MLA_KV_LORA
D_IN_P = O_KR + LANES

_NT = (((1,), (1,)), ((), ()))
BLK = 512
HALF = BLK // 2
PAIRS = 2
AHEAD = 3


def _rms(x, g):
    return x * lax.rsqrt(jnp.mean(x * x, axis=-1, keepdims=True) + RMS_EPS) * g


def _silu(x):
    return x / (1.0 + jnp.exp(-x))


def _rope128(x, c, s, below, half):
    partner = jnp.where(below, pltpu.roll(x, half, axis=1), pltpu.roll(x, LANES - half, axis=1))
    return x * c + partner * s


def _folded(c, n):
    return jnp.where(c < n // 2, c, n + n // 2 - 1 - c)


def _proj_kernel(x_ref, pos_ref, posprev_ref, tab_ref, ng_ref, w_ref, qg_ref, wuq_ref, kvg_ref,
                 wk_ref, wvt_ref, wdvt_ref,
                 dq_ref, dk_ref, dvt_ref, dg_ref, mq_ref, mk_ref, mvt_ref, mg_ref,
                 rowtab_ref, rope_ref, nb_ref, flag_ref):
    f32 = jnp.float32
    bf16 = jnp.bfloat16
    i = pl.program_id(0)
    row = lax.broadcasted_iota(jnp.int32, (BLK, LANES), 0).astype(f32)

    def store_tables(slot, t, c, s):
        rope_ref[slot, 2 * t] = c
        rope_ref[slot, 2 * t + 1] = s * tab_ref[3 * t + 1:3 * t + 2, :]

    def prep(slot):
        nb_ref[slot] = _rms(x_ref[...], ng_ref[...]).astype(bf16)
        pos = pos_ref[...]
        base = pos[0:1, :]
        flag_ref[slot] = jnp.sum(jnp.where(pos == base + row, 0.0, 1.0))
        for t in range(2):
            ang = base * tab_ref[3 * t:3 * t + 1, :]
            cb, sb = jnp.cos(ang), jnp.sin(ang)
            cr, sr = rowtab_ref[2 * t], rowtab_ref[2 * t + 1]
            store_tables(slot, t, cb * cr - sb * sr, sb * cr + cb * sr)

    def direct_tables(slot):
        pos = posprev_ref[...]
        for t in range(2):
            ang = pos * tab_ref[3 * t:3 * t + 1, :]
            store_tables(slot, t, jnp.cos(ang), jnp.sin(ang))

    def main(slot):
        nb = nb_ref[slot]
        cd, sd, below_d = rope_ref[slot, 0], rope_ref[slot, 1], tab_ref[2:3, :] > 0.0
        cm, sm, below_m = rope_ref[slot, 2], rope_ref[slot, 3], tab_ref[5:6, :] > 0.0

        def proj(off, width):
            return jnp.dot(nb, w_ref[:, off:off + width], preferred_element_type=f32)

        q_scale = DIFF_HD ** -0.5 * LOG2E
        dq = proj(O_DQ, DIFF_WIDTH)
        dk = proj(O_DK, DIFF_WIDTH)
        for c in range(DIFF_WIDTH // LANES):
            sl = slice(c * LANES, (c + 1) * LANES)
            dq_ref[0, :, sl] = (_rope128(dq[:, sl], cd, sd, below_d, DIFF_HD // 2)
                                * q_scale).astype(bf16)
            dk_ref[:, sl] = _rope128(dk[:, sl], cd, sd, below_d, DIFF_HD // 2).astype(bf16)
        dvt_ref[0] = lax.dot_general(wdvt_ref[...], nb, _NT,
                                     preferred_element_type=f32).astype(bf16)
        dg_ref[0] = _silu(proj(O_DG, DIFF_WIDTH)).astype(bf16)
        mg_ref[0] = _silu(proj(O_MG, MLA_WIDTH)).astype(bf16)

        lat = proj(O_CQ, D_IN_P - O_CQ)
        cqn = _rms(lat[:, :MLA_Q_LORA], qg_ref[...]).astype(bf16)
        mq = jnp.dot(cqn, wuq_ref[...], preferred_element_type=f32)
        m_scale = (MLA_NOPE + MLA_ROPE) ** -0.5 * LOG2E

        ckvn = _rms(lat[:, O_CKV - O_CQ:O_KR - O_CQ], kvg_ref[...]).astype(bf16)
        kn = jnp.dot(ckvn, wk_ref[...], preferred_element_type=f32)
        mvt_ref[0] = lax.dot_general(wvt_ref[...], ckvn, _NT,
                                     preferred_element_type=f32).astype(bf16)
        kr = _rope128(lat[:, O_KR - O_CQ:], cm, sm, below_m, MLA_ROPE // 2)
        for h in range(MLA_HEADS):
            sl = slice(h * LANES, (h + 1) * LANES)
            mq_ref[0, :, sl] = (_rope128(mq[:, sl], cm, sm, below_m, MLA_ROPE // 2)
                                * m_scale).astype(bf16)
            mk_ref[:, sl] = (kn[:, sl] + kr).astype(bf16)

    def step(cur):
        prv = 1 - cur

        @pl.when(flag_ref[prv] != 0.0)
        def _():
            direct_tables(prv)

        main(prv)
        prep(cur)

    @pl.when(i == 0)
    def _():
        for t in range(2):
            ang = row * tab_ref[3 * t:3 * t + 1, :]
            rowtab_ref[2 * t] = jnp.cos(ang)
            rowtab_ref[2 * t + 1] = jnp.sin(ang)
        prep(0)

    @pl.when(i % 2 == 1)
    def _():
        step(1)

    @pl.when((i > 0) & (i % 2 == 0))
    def _():
        step(0)


def _proj_call(x2, posb, tab, ng, w_in_p, qg, wuq_p, kvg, wk_p, wvt_p, wdvt, nq):
    t = x2.shape[0]
    n = t // BLK
    bf16 = jnp.bfloat16
    nxt = lambda i: jnp.minimum(i, n - 1)
    cur = lambda i: jnp.maximum(i - 1, 0)
    row_in = lambda w, blk: pl.BlockSpec((BLK, w), lambda i: (blk(i), 0))
    full = lambda a: pl.BlockSpec(a.shape, lambda i: (0, 0))
    row = lambda w: pl.BlockSpec((BLK, w), lambda i: (cur(i), 0))
    blk3 = lambda w: pl.BlockSpec((1, w, BLK), lambda i: (cur(i), 0, 0))
    fold = lambda w: pl.BlockSpec(
        (1, BLK, w), lambda i: (cur(i) // nq * nq + _folded(cur(i) % nq, nq), 0, 0))
    rows = lambda w: jax.ShapeDtypeStruct((t, w), bf16)
    blks = lambda w: jax.ShapeDtypeStruct((n, w, BLK), bf16)
    folds = lambda w: jax.ShapeDtypeStruct((n, BLK, w), bf16)
    return pl.pallas_call(
        _proj_kernel,
        grid=(n + 1,),
        in_specs=[row_in(D_MODEL, nxt), row_in(LANES, nxt), row_in(LANES, cur), full(tab),
                  full(ng), full(w_in_p), full(qg), full(wuq_p), full(kvg), full(wk_p),
                  full(wvt_p), full(wdvt)],
        out_specs=[fold(DIFF_WIDTH), row(DIFF_WIDTH), blk3(DIFF_WIDTH), fold(DIFF_WIDTH),
                   fold(MLA_HEADS * LANES), row(MLA_HEADS * LANES), blk3(MLA_WIDTH),
                   fold(MLA_WIDTH)],
        out_shape=[folds(DIFF_WIDTH), rows(DIFF_WIDTH), blks(DIFF_WIDTH), folds(DIFF_WIDTH),
                   folds(MLA_HEADS * LANES), rows(MLA_HEADS * LANES), blks(MLA_WIDTH),
                   folds(MLA_WIDTH)],
        scratch_shapes=[pltpu.VMEM((4, BLK, LANES), jnp.float32),
                        pltpu.VMEM((2, 4, BLK, LANES), jnp.float32),
                        pltpu.VMEM((2, BLK, D_MODEL), bf16),
                        pltpu.SMEM((2,), jnp.float32)],
        compiler_params=pltpu.CompilerParams(
            dimension_semantics=("arbitrary",), vmem_limit_bytes=VMEM_LIMIT),
        name="proj",
    )(x2, posb, posb, tab, ng, w_in_p, qg, wuq_p, kvg, wk_p, wvt_p, wdvt)


SUM_LO, SUM_HI = 2.0 ** -64, 2.0 ** 64
OUT_LIMIT = 2.0 ** 100


def _attn_fast(nq, get_q, get_k, vt_ref, acc_ref, l_ref, epilogue):
    f32 = jnp.float32
    first = pl.program_id(2) * PAIRS

    def diagonal(side, j):
        out = []
        for i in range(2):
            out += [(side, j, i, 0, HALF, 0, HALF, True, True),
                    (side, j, i, 0, HALF, HALF, HALF, False, True),
                    (side, j, i, HALF, HALF, HALF, HALF, True, False)]
        return out

    def full(side, j):
        return [(side, j, i, 0, BLK, q0, HALF, False, False)
                for i in range(2) for q0 in (0, HALF)]

    units, done = [], {}
    for p in range(PAIRS):
        c = first + p
        units += diagonal((p, 0), c) + diagonal((p, 1), nq - 1 - c)
        for t in range(nq - 1):
            on_side0 = t < c
            side = (p, jnp.where(on_side0, 0, 1) if t < nq // 2 - 1 else 1)
            j = jnp.where(on_side0, t, t - c)
            units += full(side, j)
            if t == nq // 2 - 2:
                done[len(units)] = (p, 0)
        done[len(units)] = (p, 1)

    def qk(u):
        side, j, i, k0, nk, q0, nq_ = units[u][:7]
        return lax.dot_general(get_k(i, j, k0, nk), get_q(side, i, q0, nq_), _NT,
                               preferred_element_type=f32)

    bad = jnp.zeros((1, BLK), f32)
    scores = {u: qk(u) for u in range(AHEAD)}
    for u, (side, j, i, k0, nk, q0, nq_, masked, init) in enumerate(units):
        s = scores.pop(u)
        if masked:
            krow = lax.broadcasted_iota(jnp.int32, s.shape, 0) + k0
            qcol = lax.broadcasted_iota(jnp.int32, s.shape, 1) + q0
            s = jnp.where(krow <= qcol, s, -jnp.inf)
        p = jnp.exp2(s)
        col = jnp.sum(p, axis=0, keepdims=True)
        pb = p.astype(jnp.bfloat16)
        if u + AHEAD < len(units):
            scores[u + AHEAD] = qk(u + AHEAD)
        pv = jnp.dot(vt_ref[0, j, :, k0:k0 + nk], pb, preferred_element_type=f32)
        where = side + (i, slice(None), slice(q0, q0 + nq_))
        if init:
            l_ref[where] = col
            acc_ref[where] = pv
        else:
            l_ref[where] += col
            acc_ref[where] += pv
        if u + 1 in done:
            bad = bad + epilogue(done[u + 1])
    return bad


def _sides():
    return [(p, s) for p in range(PAIRS) for s in range(2)]


def _attn_rescaling(nq, get_q, get_k, vt_ref, acc_ref, l_ref):
    f32 = jnp.float32
    first = pl.program_id(2) * PAIRS
    acc_ref[...] = jnp.zeros(acc_ref.shape, f32)

    for side in _sides():
        n_full = first + side[0] if side[1] == 0 else nq - 1 - first - side[0]

        def step(j, carry, masked, side=side):
            vt = vt_ref[0, j]
            out = []
            for i in range(2):
                m_prev, l_prev = carry[2 * i], carry[2 * i + 1]
                s = lax.dot_general(get_k(i, j, 0, BLK), get_q(side, i, 0, BLK), _NT,
                                    preferred_element_type=f32)
                if masked:
                    krow = lax.broadcasted_iota(jnp.int32, s.shape, 0)
                    qcol = lax.broadcasted_iota(jnp.int32, s.shape, 1)
                    s = jnp.where(krow <= qcol, s, -jnp.inf)
                m_new = jnp.maximum(m_prev, jnp.max(s, axis=0, keepdims=True))
                alpha = jnp.exp2(m_prev - m_new)
                p = jnp.exp2(s - m_new)
                l_new = alpha * l_prev + jnp.sum(p, axis=0, keepdims=True)
                acc_ref[side + (i,)] = alpha * acc_ref[side + (i,)] + jnp.dot(
                    vt, p.astype(jnp.bfloat16), preferred_element_type=f32)
                out += [m_new, l_new]
            return tuple(out)

        neg = jnp.full((1, BLK), -jnp.inf, f32)
        zero = jnp.zeros((1, BLK), f32)
        carry = lax.fori_loop(0, n_full, lambda j, cr: step(j, cr, False),
                              (neg, zero, neg, zero))
        carry = step(n_full, carry, True)
        l_ref[side + (0,)] = carry[1]
        l_ref[side + (1,)] = carry[3]


def _sums_out_of_range(l0, l1):
    ok = ((l0 > SUM_LO) & (l0 < SUM_HI)) & ((l1 > SUM_LO) & (l1 < SUM_HI))
    return jnp.where(ok, 0.0, 1.0)


def _guarded(nq, get_q, get_k, vt_ref, acc_ref, l_ref, epilogue):
    n_bad = jnp.sum(_attn_fast(nq, get_q, get_k, vt_ref, acc_ref, l_ref, epilogue))

    @pl.when(n_bad > 0.0)
    def _():
        _attn_rescaling(nq, get_q, get_k, vt_ref, acc_ref, l_ref)
        for side in _sides():
            epilogue(side)


def _key_rows(j, k0, n):
    return pl.ds(pl.multiple_of(j * BLK + k0, HALF), n)


def _diff_kernel(nq, lam_ref, q_ref, k_ref, vt_ref, g_ref, sg_ref, o_ref, qz_ref, acc_ref, l_ref):
    f32 = jnp.float32
    for p, s in _sides():
        q = q_ref[0, s, p]
        lane = lax.broadcasted_iota(jnp.int32, q.shape, 1)
        zero = jnp.zeros_like(q)
        qz_ref[p, s, 0] = jnp.where(lane < DIFF_HD, q, zero)
        qz_ref[p, s, 1] = jnp.where(lane >= DIFF_HD, q, zero)

    lam_rows = lam_ref[...]
    lam_init = 0.8 - 0.6 * math.exp(-0.3 * 0)
    lam = (jnp.exp(jnp.sum(lam_rows[0:1] * lam_rows[1:2], axis=1, keepdims=True))
           - jnp.exp(jnp.sum(lam_rows[2:3] * lam_rows[3:4], axis=1, keepdims=True))
           + lam_init)

    def epilogue(side):
        p, s = side
        l0, l1 = l_ref[p, s, 0], l_ref[p, s, 1]
        ot = acc_ref[p, s, 0] * (1.0 / l0) - acc_ref[p, s, 1] * (lam / l1)
        ms = jnp.mean(ot * ot, axis=0, keepdims=True)
        ot = ot * lax.rsqrt(ms + RMS_EPS)
        o = ot.T * (sg_ref[...] * (1.0 - lam_init)) * g_ref[0, s, p].astype(f32)
        o_ref[0, s, p] = o.astype(o_ref.dtype)
        return _sums_out_of_range(l0, l1) + jnp.where(ms < OUT_LIMIT, 0.0, 1.0)

    _guarded(nq,
             lambda side, i, q0, n: qz_ref[side[0], side[1], i, q0:q0 + n, :],
             lambda i, j, k0, n: k_ref[0, _key_rows(j, k0, n), :],
             vt_ref, acc_ref, l_ref, epilogue)


def _mla_kernel(nq, q_ref, k_ref, vt_ref, g_ref, o_ref, acc_ref, l_ref):
    f32 = jnp.float32

    def epilogue(side):
        p, s = side
        l0, l1 = l_ref[p, s, 0], l_ref[p, s, 1]
        feat = lax.broadcasted_iota(jnp.int32, (LANES, BLK), 0)
        ot = jnp.where(feat < MLA_V, acc_ref[p, s, 0] * (1.0 / l0), acc_ref[p, s, 1] * (1.0 / l1))
        peak = jnp.max(jnp.abs(ot), axis=0, keepdims=True)
        o_ref[0, s, p] = (ot.T * g_ref[0, s, p].astype(f32)).astype(o_ref.dtype)
        return _sums_out_of_range(l0, l1) + jnp.where(peak < OUT_LIMIT, 0.0, 1.0)

    _guarded(nq,
             lambda side, i, q0, n: q_ref[0, side[1], side[0], q0:q0 + n,
                                          i * LANES:(i + 1) * LANES],
             lambda i, j, k0, n: k_ref[0, _key_rows(j, k0, n), i * LANES:(i + 1) * LANES],
             vt_ref, acc_ref, l_ref, epilogue)


def _attn_scratch():
    return [pltpu.VMEM((PAIRS, 2, 2, LANES, BLK), jnp.float32),
            pltpu.VMEM((PAIRS, 2, 2, 1, BLK), jnp.float32)]


def _pair_spec(width):
    return pl.BlockSpec((1, 2, PAIRS, BLK, width), lambda bi, h, c: (bi, 0, c, 0, h))


def _diff_call(lam_rows, dq, dk, dvt, dg, subg):
    b, _, npair, _, _ = dq.shape
    nq = 2 * npair
    s = nq * BLK
    kspec = pl.BlockSpec((1, s, LANES), lambda bi, h, c: (bi, 0, h))
    vtspec = pl.BlockSpec((1, nq, LANES, BLK), lambda bi, h, c: (bi, 0, h, 0))
    return pl.pallas_call(
        functools.partial(_diff_kernel, nq),
        grid=(b, DIFF_HEADS, npair // PAIRS),
        in_specs=[pl.BlockSpec(lam_rows.shape, lambda bi, h, c: (0, 0)),
                  _pair_spec(LANES), kspec, vtspec, _pair_spec(LANES),
                  pl.BlockSpec(subg.shape, lambda bi, h, c: (0, 0))],
        out_specs=_pair_spec(LANES),
        out_shape=jax.ShapeDtypeStruct(dq.shape, jnp.bfloat16),
        scratch_shapes=[pltpu.VMEM((PAIRS, 2, 2, BLK, LANES), jnp.bfloat16)] + _attn_scratch(),
        compiler_params=pltpu.CompilerParams(
            dimension_semantics=("arbitrary",) * 3, vmem_limit_bytes=VMEM_LIMIT),
        name="diffattn",
    )(lam_rows, dq, dk, dvt, dg, subg)


def _mla_call(mq, mk, mvt, mg):
    b, _, npair, _, _ = mq.shape
    nq = 2 * npair
    s = nq * BLK
    return pl.pallas_call(
        functools.partial(_mla_kernel, nq),
        grid=(b, MLA_HEADS // 2, npair // PAIRS),
        in_specs=[_pair_spec(2 * LANES),
                  pl.BlockSpec((1, s, 2 * LANES), lambda bi, h, c: (bi, 0, h)),
                  pl.BlockSpec((1, nq, LANES, BLK), lambda bi, h, c: (bi, 0, h, 0)),
                  _pair_spec(LANES)],
        out_specs=_pair_spec(LANES),
        out_shape=jax.ShapeDtypeStruct(mg.shape, jnp.bfloat16),
        scratch_shapes=_attn_scratch(),
        compiler_params=pltpu.CompilerParams(
            dimension_semantics=("arbitrary",) * 3, vmem_limit_bytes=VMEM_LIMIT),
        name="mlaattn",
    )(mq, mk, mvt, mg)


def _out_kernel(x_ref, od_ref, om_ref, p_ref, wo_ref, wp_ref, wg_ref, fg_ref, o_ref):
    f32 = jnp.float32
    bf16 = jnp.bfloat16
    for r in (slice(0, HALF), slice(HALF, BLK)):
        h = (x_ref[r, :]
             + jnp.dot(od_ref[0, r, :], wo_ref[0:DIFF_WIDTH, :], preferred_element_type=f32)
             + jnp.dot(om_ref[0, r, :], wo_ref[DIFF_WIDTH:, :], preferred_element_type=f32))
        gate = jax.nn.sigmoid(jnp.dot(h.astype(bf16), wg_ref[...], preferred_element_type=f32))
        emb = jnp.dot(p_ref[r, :].astype(bf16), wp_ref[...], preferred_element_type=f32)
        h = h + emb * gate
        o_ref[r, :] = _rms(h, fg_ref[...])


def _out_call(x2, od, om, p2, wo, wp, wg, fg, nq):
    t = x2.shape[0]
    row = lambda w: pl.BlockSpec((BLK, w), lambda i: (i, 0))
    full = lambda a: pl.BlockSpec(a.shape, lambda i: (0, 0))
    fold = lambda w: pl.BlockSpec((1, BLK, w), lambda i: (i // nq * nq + _folded(i % nq, nq), 0, 0))
    return pl.pallas_call(
        _out_kernel,
        grid=(t // BLK,),
        in_specs=[row(D_MODEL), fold(DIFF_WIDTH), fold(MLA_WIDTH), row(PLE_DIM),
                  full(wo), full(wp), full(wg), full(fg)],
        out_specs=row(D_MODEL),
        out_shape=jax.ShapeDtypeStruct((t, D_MODEL), jnp.float32),
        compiler_params=pltpu.CompilerParams(
            dimension_semantics=("arbitrary",), vmem_limit_bytes=VMEM_LIMIT),
        name="outproj",
    )(x2, od, om, p2, wo, wp, wg, fg)


def _rope_tables():
    f32 = jnp.float32
    lane = jnp.arange(LANES)
    inv_d = ROPE_THETA ** (-jnp.arange(0, DIFF_HD, 2, dtype=f32) / DIFF_HD)
    half_d = DIFF_HD // 2
    inv_d_l = inv_d[lane % half_d]
    sp_d = jnp.where(lane % DIFF_HD >= half_d, 1.0, 0.0)
    sm_d = jnp.where(lane % DIFF_HD < half_d, -1.0, 0.0)
    inv_m = ROPE_THETA ** (-jnp.arange(0, MLA_ROPE, 2, dtype=f32) / MLA_ROPE)
    half_m = MLA_ROPE // 2
    in_rope = (lane >= MLA_NOPE) & (lane < MLA_NOPE + MLA_ROPE)
    inv_m_l = jnp.where(in_rope, inv_m[(lane - MLA_NOPE) % half_m], 0.0)
    sp_m = jnp.where(in_rope & (lane >= MLA_NOPE + half_m), 1.0, 0.0)
    sm_m = jnp.where(in_rope & (lane < MLA_NOPE + half_m), -1.0, 0.0)
    zero = jnp.zeros((LANES,), f32)
    return jnp.stack([inv_d_l, sp_d + sm_d, sp_d, inv_m_l, sp_m + sm_m, sp_m, zero, zero]).astype(f32)


def kernel(x, p, positions, norm_g, w_in, diff_lambda, diff_subln_g, mla_q_norm_g, w_uq,
           mla_kv_norm_g, w_ukv, w_out, w_ple, w_ple_gate, final_norm_g):
    b, s, d = x.shape
    t = b * s
    nq = s // BLK
    assert s % (2 * PAIRS * BLK) == 0
    f32 = jnp.float32
    bf16 = jnp.bfloat16
    i = 0

    offs = [0, 512, 1024, 1536, 2048, 2432, 2560, 2592, 3104]
    seg = [w_in[i][:, offs[j]:offs[j + 1]] for j in range(8)]
    wdq, wdk, wdv, wdg, wcq, wckv, wkr, wmg = seg
    zc = lambda n: jnp.zeros((d, n), f32)
    w_in_p = jnp.concatenate(
        [wdq, wdk, wdg, wmg, wcq, wckv, zc(MLA_NOPE), wkr, zc(LANES - MLA_NOPE - MLA_ROPE)],
        axis=1).astype(bf16)
    wuq_p = jnp.pad(w_uq[i].reshape(MLA_Q_LORA, MLA_HEADS, MLA_NOPE + MLA_ROPE),
                    ((0, 0), (0, 0), (0, LANES - MLA_NOPE - MLA_ROPE))
                    ).reshape(MLA_Q_LORA, MLA_HEADS * LANES).astype(bf16)
    wkv3 = w_ukv[i].reshape(MLA_KV_LORA, MLA_HEADS, MLA_NOPE + MLA_V)
    wk_p = jnp.pad(wkv3[:, :, :MLA_NOPE], ((0, 0), (0, 0), (0, LANES - MLA_NOPE))
                   ).reshape(MLA_KV_LORA, MLA_HEADS * LANES).astype(bf16)
    wvt_p = wkv3[:, :, MLA_NOPE:].reshape(MLA_KV_LORA, MLA_WIDTH).T.astype(bf16)
    wdvt = wdv.T.astype(bf16)

    x2 = x.reshape(t, d)
    posb = jnp.broadcast_to(positions.astype(f32).reshape(t, 1), (t, LANES))
    dq, dk, dvt, dg, mq, mk, mvt, mg = _proj_call(
        x2, posb, _rope_tables(), norm_g[i].reshape(1, d), w_in_p,
        mla_q_norm_g[i].reshape(1, MLA_Q_LORA), wuq_p,
        mla_kv_norm_g[i].reshape(1, MLA_KV_LORA), wk_p, wvt_p, wdvt, nq)

    r3 = lambda a: a.reshape(b, s, a.shape[-1])
    r4 = lambda a: a.reshape(b, nq, a.shape[-2], BLK)
    r5 = lambda a: a.reshape(b, 2, nq // 2, BLK, a.shape[-1])
    od = _diff_call(diff_lambda[i].astype(f32), r5(dq), r3(dk), r4(dvt), r5(dg),
                    diff_subln_g[i].reshape(1, 2 * DIFF_HD))
    om = _mla_call(r5(mq), r3(mk), r4(mvt), r5(mg))

    out = _out_call(x2, od.reshape(t // BLK, BLK, DIFF_WIDTH), om.reshape(t // BLK, BLK, MLA_WIDTH),
                    p[i].reshape(t, PLE_DIM), w_out[i].astype(bf16), w_ple[i].astype(bf16),
                    w_ple_gate[i].astype(bf16), final_norm_g.reshape(1, d), nq)
    return out.reshape(b, s, d)
```

```python
import functools
import math

import jax
import jax.numpy as jnp
from jax import lax
from jax.experimental import pallas as pl
from jax.experimental.pallas import tpu as pltpu

D_MODEL = 1024
PLE_DIM = 256
ROPE_THETA = 10000.0
RMS_EPS = 1e-6

DIFF_WIDTH = 512
DIFF_HEADS = 4
DIFF_HD = 64
MLA_WIDTH = 512
MLA_HEADS = 8
MLA_NOPE = 64
MLA_ROPE = 32
MLA_V = 64
MLA_Q_LORA = 384
MLA_KV_LORA = 128

LANES = 128
LOG2E = math.log2(math.e)
VMEM_LIMIT = 48 * 1024 * 1024

O_DQ, O_DK, O_DG, O_MG = 0, 512, 1024, 1536
O_CQ = 2048
O_CKV = O_CQ + MLA_Q_LORA
O_KR = O_CKV + MLA_KV_LORA
D_IN_P = O_KR + LANES

_NT = (((1,), (1,)), ((), ()))
BLK = 512
HALF = BLK // 2
PAIRS = 2
AHEAD = 3


def _rms(x, g):
    return x * lax.rsqrt(jnp.mean(x * x, axis=-1, keepdims=True) + RMS_EPS) * g


def _silu(x):
    return x / (1.0 + jnp.exp(-x))


def _rope128(x, c, s, below, half):
    partner = jnp.where(below, pltpu.roll(x, half, axis=1), pltpu.roll(x, LANES - half, axis=1))
    return x * c + partner * s


def _folded(c, n):
    return jnp.where(c < n // 2, c, n + n // 2 - 1 - c)


def _proj_kernel(x_ref, pos_ref, posprev_ref, tab_ref, ng_ref, w_ref, qg_ref, wuq_ref, kvg_ref,
                 wk_ref, wvt_ref, wdvt_ref,
                 dq_ref, dk_ref, dvt_ref, dg_ref, mq_ref, mk_ref, mvt_ref, mg_ref,
                 rowtab_ref, rope_ref, nb_ref, flag_ref):
    f32 = jnp.float32
    bf16 = jnp.bfloat16
    i = pl.program_id(0)
    row = lax.broadcasted_iota(jnp.int32, (BLK, LANES), 0).astype(f32)

    def store_tables(slot, t, c, s):
        rope_ref[slot, 2 * t] = c
        rope_ref[slot, 2 * t + 1] = s * tab_ref[3 * t + 1:3 * t + 2, :]

    def prep(slot):
        nb_ref[slot] = _rms(x_ref[...], ng_ref[...]).astype(bf16)
        pos = pos_ref[...]
        base = pos[0:1, :]
        flag_ref[slot] = jnp.sum(jnp.where(pos == base + row, 0.0, 1.0))
        for t in range(2):
            ang = base * tab_ref[3 * t:3 * t + 1, :]
            cb, sb = jnp.cos(ang), jnp.sin(ang)
            cr, sr = rowtab_ref[2 * t], rowtab_ref[2 * t + 1]
            store_tables(slot, t, cb * cr - sb * sr, sb * cr + cb * sr)

    def direct_tables(slot):
        pos = posprev_ref[...]
        for t in range(2):
            ang = pos * tab_ref[3 * t:3 * t + 1, :]
            store_tables(slot, t, jnp.cos(ang), jnp.sin(ang))

    def main(slot):
        nb = nb_ref[slot]
        cd, sd, below_d = rope_ref[slot, 0], rope_ref[slot, 1], tab_ref[2:3, :] > 0.0
        cm, sm, below_m = rope_ref[slot, 2], rope_ref[slot, 3], tab_ref[5:6, :] > 0.0

        def proj(off, width):
            return jnp.dot(nb, w_ref[:, off:off + width], preferred_element_type=f32)

        lat = proj(O_CQ, D_IN_P - O_CQ)
        q_scale = DIFF_HD ** -0.5 * LOG2E
        dq = proj(O_DQ, DIFF_WIDTH)
        dk = proj(O_DK, DIFF_WIDTH)
        cqn = _rms(lat[:, :MLA_Q_LORA], qg_ref[...]).astype(bf16)
        ckvn = _rms(lat[:, O_CKV - O_CQ:O_KR - O_CQ], kvg_ref[...]).astype(bf16)
        for c in range(DIFF_WIDTH // LANES):
            sl = slice(c * LANES, (c + 1) * LANES)
            dq_ref[0, :, sl] = (_rope128(dq[:, sl], cd, sd, below_d, DIFF_HD // 2)
                                * q_scale).astype(bf16)
            dk_ref[:, sl] = _rope128(dk[:, sl], cd, sd, below_d, DIFF_HD // 2).astype(bf16)

        mq = jnp.dot(cqn, wuq_ref[...], preferred_element_type=f32)
        m_scale = (MLA_NOPE + MLA_ROPE) ** -0.5 * LOG2E
        kn = jnp.dot(ckvn, wk_ref[...], preferred_element_type=f32)
        kr = _rope128(lat[:, O_KR - O_CQ:], cm, sm, below_m, MLA_ROPE // 2)
        for h in range(MLA_HEADS):
            sl = slice(h * LANES, (h + 1) * LANES)
            mq_ref[0, :, sl] = (_rope128(mq[:, sl], cm, sm, below_m, MLA_ROPE // 2)
                                * m_scale).astype(bf16)
            mk_ref[:, sl] = (kn[:, sl] + kr).astype(bf16)

        dg_ref[0] = _silu(proj(O_DG, DIFF_WIDTH)).astype(bf16)
        mg_ref[0] = _silu(proj(O_MG, MLA_WIDTH)).astype(bf16)
        mvt_ref[0] = lax.dot_general(wvt_ref[...], ckvn, _NT,
                                     preferred_element_type=f32).astype(bf16)
        dvt_ref[0] = lax.dot_general(wdvt_ref[...], nb, _NT,
                                     preferred_element_type=f32).astype(bf16)

    def step(cur):
        prv = 1 - cur

        @pl.when(flag_ref[prv] != 0.0)
        def _():
            direct_tables(prv)

        main(prv)
        prep(cur)

    @pl.when(i == 0)
    def _():
        for t in range(2):
            ang = row * tab_ref[3 * t:3 * t + 1, :]
            rowtab_ref[2 * t] = jnp.cos(ang)
            rowtab_ref[2 * t + 1] = jnp.sin(ang)
        prep(0)

    @pl.when(i % 2 == 1)
    def _():
        step(1)

    @pl.when((i > 0) & (i % 2 == 0))
    def _():
        step(0)


def _proj_call(x2, posb, tab, ng, w_in_p, qg, wuq_p, kvg, wk_p, wvt_p, wdvt, nq):
    t = x2.shape[0]
    n = t // BLK
    bf16 = jnp.bfloat16
    nxt = lambda i: jnp.minimum(i, n - 1)
    cur = lambda i: jnp.maximum(i - 1, 0)
    row_in = lambda w, blk: pl.BlockSpec((BLK, w), lambda i: (blk(i), 0))
    full = lambda a: pl.BlockSpec(a.shape, lambda i: (0, 0))
    row = lambda w: pl.BlockSpec((BLK, w), lambda i: (cur(i), 0))
    blk3 = lambda w: pl.BlockSpec((1, w, BLK), lambda i: (cur(i), 0, 0))
    fold = lambda w: pl.BlockSpec(
        (1, BLK, w), lambda i: (cur(i) // nq * nq + _folded(cur(i) % nq, nq), 0, 0))
    rows = lambda w: jax.ShapeDtypeStruct((t, w), bf16)
    blks = lambda w: jax.ShapeDtypeStruct((n, w, BLK), bf16)
    folds = lambda w: jax.ShapeDtypeStruct((n, BLK, w), bf16)
    return pl.pallas_call(
        _proj_kernel,
        grid=(n + 1,),
        in_specs=[row_in(D_MODEL, nxt), row_in(LANES, nxt), row_in(LANES, cur), full(tab),
                  full(ng), full(w_in_p), full(qg), full(wuq_p), full(kvg), full(wk_p),
                  full(wvt_p), full(wdvt)],
        out_specs=[fold(DIFF_WIDTH), row(DIFF_WIDTH), blk3(DIFF_WIDTH), fold(DIFF_WIDTH),
                   fold(MLA_HEADS * LANES), row(MLA_HEADS * LANES), blk3(MLA_WIDTH),
                   fold(MLA_WIDTH)],
        out_shape=[folds(DIFF_WIDTH), rows(DIFF_WIDTH), blks(DIFF_WIDTH), folds(DIFF_WIDTH),
                   folds(MLA_HEADS * LANES), rows(MLA_HEADS * LANES), blks(MLA_WIDTH),
                   folds(MLA_WIDTH)],
        scratch_shapes=[pltpu.VMEM((4, BLK, LANES), jnp.float32),
                        pltpu.VMEM((2, 4, BLK, LANES), jnp.float32),
                        pltpu.VMEM((2, BLK, D_MODEL), bf16),
                        pltpu.SMEM((2,), jnp.float32)],
        compiler_params=pltpu.CompilerParams(
            dimension_semantics=("arbitrary",), vmem_limit_bytes=VMEM_LIMIT),
        name="proj",
    )(x2, posb, posb, tab, ng, w_in_p, qg, wuq_p, kvg, wk_p, wvt_p, wdvt)


SUM_LO, SUM_HI = 2.0 ** -64, 2.0 ** 64
OUT_LIMIT = 2.0 ** 100


def _attn_fast(nq, get_q, get_k, vt_ref, acc_ref, l_ref, epilogue):
    f32 = jnp.float32
    first = pl.program_id(2) * PAIRS

    _zero(acc_ref, l_ref)

    def diagonal(side, j):
        out = []
        for i in range(2):
            out += [(side, j, i, 0, HALF, 0, HALF, True),
                    (side, j, i, 0, HALF, HALF, HALF, False),
                    (side, j, i, HALF, HALF, HALF, HALF, True)]
        return out

    def full(side, j):
        return [(side, j, i, 0, BLK, q0, HALF, False) for i in range(2) for q0 in (0, HALF)]

    units, done = [], {}
    for p in range(PAIRS):
        c = first + p
        for t in range(nq // 2 - 1):
            on_side0 = t < c
            units += full((p, jnp.where(on_side0, 0, 1)), jnp.where(on_side0, t, t - c))
        units += diagonal((p, 0), c)
        done[len(units)] = (p, 0)
        units += diagonal((p, 1), nq - 1 - c)
        for t in range(nq // 2 - 1, nq - 1):
            units += full((p, 1), t - c)
        done[len(units)] = (p, 1)

    def qk(u):
        side, j, i, k0, nk, q0, nq_ = units[u][:7]
        return lax.dot_general(get_k(i, j, k0, nk), get_q(side, i, q0, nq_), _NT,
                               preferred_element_type=f32)

    bad = jnp.zeros((1, BLK), f32)
    scores = {u: qk(u) for u in range(AHEAD)}
    for u, (side, j, i, k0, nk, q0, nq_, masked) in enumerate(units):
        s = scores.pop(u)
        if masked:
            krow = lax.broadcasted_iota(jnp.int32, s.shape, 0) + k0
            qcol = lax.broadcasted_iota(jnp.int32, s.shape, 1) + q0
            s = jnp.where(krow <= qcol, s, -jnp.inf)
        p = jnp.exp2(s)
        col = jnp.sum(p, axis=0, keepdims=True)
        pb = p.astype(jnp.bfloat16)
        if u + AHEAD < len(units):
            scores[u + AHEAD] = qk(u + AHEAD)
        pv = jnp.dot(vt_ref[0, j, :, k0:k0 + nk], pb, preferred_element_type=f32)
        where = side + (i, slice(None), slice(q0, q0 + nq_))
        l_ref[where] += col
        acc_ref[where] += pv
        if u + 1 in done:
            bad = bad + epilogue(done[u + 1])
    return bad


def _sides():
    return [(p, s) for p in range(PAIRS) for s in range(2)]


def _attn_rescaling(nq, get_q, get_k, vt_ref, acc_ref, l_ref):
    f32 = jnp.float32
    first = pl.program_id(2) * PAIRS
    acc_ref[...] = jnp.zeros(acc_ref.shape, f32)

    for side in _sides():
        n_full = first + side[0] if side[1] == 0 else nq - 1 - first - side[0]

        def step(j, carry, masked, side=side):
            vt = vt_ref[0, j]
            out = []
            for i in range(2):
                m_prev, l_prev = carry[2 * i], carry[2 * i + 1]
                s = lax.dot_general(get_k(i, j, 0, BLK), get_q(side, i, 0, BLK), _NT,
                                    preferred_element_type=f32)
                if masked:
                    krow = lax.broadcasted_iota(jnp.int32, s.shape, 0)
                    qcol = lax.broadcasted_iota(jnp.int32, s.shape, 1)
                    s = jnp.where(krow <= qcol, s, -jnp.inf)
                m_new = jnp.maximum(m_prev, jnp.max(s, axis=0, keepdims=True))
                alpha = jnp.exp2(m_prev - m_new)
                p = jnp.exp2(s - m_new)
                l_new = alpha * l_prev + jnp.sum(p, axis=0, keepdims=True)
                acc_ref[side + (i,)] = alpha * acc_ref[side + (i,)] + jnp.dot(
                    vt, p.astype(jnp.bfloat16), preferred_element_type=f32)
                out += [m_new, l_new]
            return tuple(out)

        neg = jnp.full((1, BLK), -jnp.inf, f32)
        zero = jnp.zeros((1, BLK), f32)
        carry = lax.fori_loop(0, n_full, lambda j, cr: step(j, cr, False),
                              (neg, zero, neg, zero))
        carry = step(n_full, carry, True)
        l_ref[side + (0,)] = carry[1]
        l_ref[side + (1,)] = carry[3]


def _sums_out_of_range(l0, l1):
    ok = ((l0 > SUM_LO) & (l0 < SUM_HI)) & ((l1 > SUM_LO) & (l1 < SUM_HI))
    return jnp.where(ok, 0.0, 1.0)


def _guarded(nq, get_q, get_k, vt_ref, acc_ref, l_ref, epilogue):
    n_bad = jnp.sum(_attn_fast(nq, get_q, get_k, vt_ref, acc_ref, l_ref, epilogue))

    @pl.when(n_bad > 0.0)
    def _():
        _attn_rescaling(nq, get_q, get_k, vt_ref, acc_ref, l_ref)
        for side in _sides():
            epilogue(side)


def _zero(*refs):
    for ref in refs:
        ref[...] = jnp.zeros(ref.shape, ref.dtype)


def _key_rows(j, k0, n):
    return pl.ds(pl.multiple_of(j * BLK + k0, HALF), n)


def _diff_kernel(nq, lam_ref, q_ref, k_ref, vt_ref, g_ref, sg_ref, o_ref, qz_ref, acc_ref, l_ref):
    f32 = jnp.float32
    for p, s in _sides():
        q = q_ref[0, s, p]
        lane = lax.broadcasted_iota(jnp.int32, q.shape, 1)
        zero = jnp.zeros_like(q)
        qz_ref[p, s, 0] = jnp.where(lane < DIFF_HD, q, zero)
        qz_ref[p, s, 1] = jnp.where(lane >= DIFF_HD, q, zero)

    lam_rows = lam_ref[...]
    lam_init = 0.8 - 0.6 * math.exp(-0.3 * 0)
    lam = (jnp.exp(jnp.sum(lam_rows[0:1] * lam_rows[1:2], axis=1, keepdims=True))
           - jnp.exp(jnp.sum(lam_rows[2:3] * lam_rows[3:4], axis=1, keepdims=True))
           + lam_init)

    def epilogue(side):
        p, s = side
        l0, l1 = l_ref[p, s, 0], l_ref[p, s, 1]
        ot = acc_ref[p, s, 0] * (1.0 / l0) - acc_ref[p, s, 1] * (lam / l1)
        ms = jnp.mean(ot * ot, axis=0, keepdims=True)
        ot = ot * lax.rsqrt(ms + RMS_EPS)
        o = ot.T * (sg_ref[...] * (1.0 - lam_init)) * g_ref[0, s, p].astype(f32)
        o_ref[0, s, p] = o.astype(o_ref.dtype)
        return _sums_out_of_range(l0, l1) + jnp.where(ms < OUT_LIMIT, 0.0, 1.0)

    _guarded(nq,
             lambda side, i, q0, n: qz_ref[side[0], side[1], i, q0:q0 + n, :],
             lambda i, j, k0, n: k_ref[0, _key_rows(j, k0, n), :],
             vt_ref, acc_ref, l_ref, epilogue)


def _mla_kernel(nq, q_ref, k_ref, vt_ref, g_ref, o_ref, acc_ref, l_ref):
    f32 = jnp.float32

    def epilogue(side):
        p, s = side
        l0, l1 = l_ref[p, s, 0], l_ref[p, s, 1]
        feat = lax.broadcasted_iota(jnp.int32, (LANES, BLK), 0)
        ot = jnp.where(feat < MLA_V, acc_ref[p, s, 0] * (1.0 / l0), acc_ref[p, s, 1] * (1.0 / l1))
        peak = jnp.max(jnp.abs(ot), axis=0, keepdims=True)
        o_ref[0, s, p] = (ot.T * g_ref[0, s, p].astype(f32)).astype(o_ref.dtype)
        return _sums_out_of_range(l0, l1) + jnp.where(peak < OUT_LIMIT, 0.0, 1.0)

    _guarded(nq,
             lambda side, i, q0, n: q_ref[0, side[1], side[0], q0:q0 + n,
                                          i * LANES:(i + 1) * LANES],
             lambda i, j, k0, n: k_ref[0, _key_rows(j, k0, n), i * LANES:(i + 1) * LANES],
             vt_ref, acc_ref, l_ref, epilogue)


def _attn_scratch():
    return [pltpu.VMEM((PAIRS, 2, 2, LANES, BLK), jnp.float32),
            pltpu.VMEM((PAIRS, 2, 2, 1, BLK), jnp.float32)]


def _pair_spec(width):
    return pl.BlockSpec((1, 2, PAIRS, BLK, width), lambda bi, h, c: (bi, 0, c, 0, h))


def _diff_call(lam_rows, dq, dk, dvt, dg, subg):
    b, _, npair, _, _ = dq.shape
    nq = 2 * npair
    s = nq * BLK
    kspec = pl.BlockSpec((1, s, LANES), lambda bi, h, c: (bi, 0, h))
    vtspec = pl.BlockSpec((1, nq, LANES, BLK), lambda bi, h, c: (bi, 0, h, 0))
    return pl.pallas_call(
        functools.partial(_diff_kernel, nq),
        grid=(b, DIFF_HEADS, npair // PAIRS),
        in_specs=[pl.BlockSpec(lam_rows.shape, lambda bi, h, c: (0, 0)),
                  _pair_spec(LANES), kspec, vtspec, _pair_spec(LANES),
                  pl.BlockSpec(subg.shape, lambda bi, h, c: (0, 0))],
        out_specs=_pair_spec(LANES),
        out_shape=jax.ShapeDtypeStruct(dq.shape, jnp.bfloat16),
        scratch_shapes=[pltpu.VMEM((PAIRS, 2, 2, BLK, LANES), jnp.bfloat16)] + _attn_scratch(),
        compiler_params=pltpu.CompilerParams(
            dimension_semantics=("arbitrary",) * 3, vmem_limit_bytes=VMEM_LIMIT),
        name="diffattn",
    )(lam_rows, dq, dk, dvt, dg, subg)


def _mla_call(mq, mk, mvt, mg):
    b, _, npair, _, _ = mq.shape
    nq = 2 * npair
    s = nq * BLK
    return pl.pallas_call(
        functools.partial(_mla_kernel, nq),
        grid=(b, MLA_HEADS // 2, npair // PAIRS),
        in_specs=[_pair_spec(2 * LANES),
                  pl.BlockSpec((1, s, 2 * LANES), lambda bi, h, c: (bi, 0, h)),
                  pl.BlockSpec((1, nq, LANES, BLK), lambda bi, h, c: (bi, 0, h, 0)),
                  _pair_spec(LANES)],
        out_specs=_pair_spec(LANES),
        out_shape=jax.ShapeDtypeStruct(mg.shape, jnp.bfloat16),
        scratch_shapes=_attn_scratch(),
        compiler_params=pltpu.CompilerParams(
            dimension_semantics=("arbitrary",) * 3, vmem_limit_bytes=VMEM_LIMIT),
        name="mlaattn",
    )(mq, mk, mvt, mg)


def _out_kernel(x_ref, od_ref, om_ref, p_ref, wo_ref, wp_ref, wg_ref, fg_ref, o_ref):
    f32 = jnp.float32
    bf16 = jnp.bfloat16
    for r in (slice(0, HALF), slice(HALF, BLK)):
        h = (x_ref[r, :]
             + jnp.dot(od_ref[0, r, :], wo_ref[0:DIFF_WIDTH, :], preferred_element_type=f32)
             + jnp.dot(om_ref[0, r, :], wo_ref[DIFF_WIDTH:, :], preferred_element_type=f32))
        gate = jax.nn.sigmoid(jnp.dot(h.astype(bf16), wg_ref[...], preferred_element_type=f32))
        emb = jnp.dot(p_ref[r, :].astype(bf16), wp_ref[...], preferred_element_type=f32)
        h = h + emb * gate
        o_ref[r, :] = _rms(h, fg_ref[...])


def _out_call(x2, od, om, p2, wo, wp, wg, fg, nq):
    t = x2.shape[0]
    row = lambda w: pl.BlockSpec((BLK, w), lambda i: (i, 0))
    full = lambda a: pl.BlockSpec(a.shape, lambda i: (0, 0))
    fold = lambda w: pl.BlockSpec((1, BLK, w), lambda i: (i // nq * nq + _folded(i % nq, nq), 0, 0))
    return pl.pallas_call(
        _out_kernel,
        grid=(t // BLK,),
        in_specs=[row(D_MODEL), fold(DIFF_WIDTH), fold(MLA_WIDTH), row(PLE_DIM),
                  full(wo), full(wp), full(wg), full(fg)],
        out_specs=row(D_MODEL),
        out_shape=jax.ShapeDtypeStruct((t, D_MODEL), jnp.float32),
        compiler_params=pltpu.CompilerParams(
            dimension_semantics=("arbitrary",), vmem_limit_bytes=VMEM_LIMIT),
        name="outproj",
    )(x2, od, om, p2, wo, wp, wg, fg)


def _rope_tables():
    f32 = jnp.float32
    lane = jnp.arange(LANES)
    inv_d = ROPE_THETA ** (-jnp.arange(0, DIFF_HD, 2, dtype=f32) / DIFF_HD)
    half_d = DIFF_HD // 2
    inv_d_l = inv_d[lane % half_d]
    sp_d = jnp.where(lane % DIFF_HD >= half_d, 1.0, 0.0)
    sm_d = jnp.where(lane % DIFF_HD < half_d, -1.0, 0.0)
    inv_m = ROPE_THETA ** (-jnp.arange(0, MLA_ROPE, 2, dtype=f32) / MLA_ROPE)
    half_m = MLA_ROPE // 2
    in_rope = (lane >= MLA_NOPE) & (lane < MLA_NOPE + MLA_ROPE)
    inv_m_l = jnp.where(in_rope, inv_m[(lane - MLA_NOPE) % half_m], 0.0)
    sp_m = jnp.where(in_rope & (lane >= MLA_NOPE + half_m), 1.0, 0.0)
    sm_m = jnp.where(in_rope & (lane < MLA_NOPE + half_m), -1.0, 0.0)
    zero = jnp.zeros((LANES,), f32)
    return jnp.stack([inv_d_l, sp_d + sm_d, sp_d, inv_m_l, sp_m + sm_m, sp_m, zero, zero]).astype(f32)


def kernel(x, p, positions, norm_g, w_in, diff_lambda, diff_subln_g, mla_q_norm_g, w_uq,
           mla_kv_norm_g, w_ukv, w_out, w_ple, w_ple_gate, final_norm_g):
    b, s, d = x.shape
    t = b * s
    nq = s // BLK
    assert s % (2 * PAIRS * BLK) == 0
    f32 = jnp.float32
    bf16 = jnp.bfloat16
    i = 0

    offs = [0, 512, 1024, 1536, 2048, 2432, 2560, 2592, 3104]
    seg = [w_in[i][:, offs[j]:offs[j + 1]] for j in range(8)]
    wdq, wdk, wdv, wdg, wcq, wckv, wkr, wmg = seg
    zc = lambda n: jnp.zeros((d, n), f32)
    w_in_p = jnp.concatenate(
        [wdq, wdk, wdg, wmg, wcq, wckv, zc(MLA_NOPE), wkr, zc(LANES - MLA_NOPE - MLA_ROPE)],
        axis=1).astype(bf16)
    wuq_p = jnp.pad(w_uq[i].reshape(MLA_Q_LORA, MLA_HEADS, MLA_NOPE + MLA_ROPE),
                    ((0, 0), (0, 0), (0, LANES - MLA_NOPE - MLA_ROPE))
                    ).reshape(MLA_Q_LORA, MLA_HEADS * LANES).astype(bf16)
    wkv3 = w_ukv[i].reshape(MLA_KV_LORA, MLA_HEADS, MLA_NOPE + MLA_V)
    wk_p = jnp.pad(wkv3[:, :, :MLA_NOPE], ((0, 0), (0, 0), (0, LANES - MLA_NOPE))
                   ).reshape(MLA_KV_LORA, MLA_HEADS * LANES).astype(bf16)
    wvt_p = wkv3[:, :, MLA_NOPE:].reshape(MLA_KV_LORA, MLA_WIDTH).T.astype(bf16)
    wdvt = wdv.T.astype(bf16)

    x2 = x.reshape(t, d)
    posb = jnp.broadcast_to(positions.astype(f32).reshape(t, 1), (t, LANES))
    dq, dk, dvt, dg, mq, mk, mvt, mg = _proj_call(
        x2, posb, _rope_tables(), norm_g[i].reshape(1, d), w_in_p,
        mla_q_norm_g[i].reshape(1, MLA_Q_LORA), wuq_p,
        mla_kv_norm_g[i].reshape(1, MLA_KV_LORA), wk_p, wvt_p, wdvt, nq)

    r3 = lambda a: a.reshape(b, s, a.shape[-1])
    r4 = lambda a: a.reshape(b, nq, a.shape[-2], BLK)
    r5 = lambda a: a.reshape(b, 2, nq // 2, BLK, a.shape[-1])
    od = _diff_call(diff_lambda[i].astype(f32), r5(dq), r3(dk), r4(dvt), r5(dg),
                    diff_subln_g[i].reshape(1, 2 * DIFF_HD))
    om = _mla_call(r5(mq), r3(mk), r4(mvt), r5(mg))

    out = _out_call(x2, od.reshape(t // BLK, BLK, DIFF_WIDTH), om.reshape(t // BLK, BLK, MLA_WIDTH),
                    p[i].reshape(t, PLE_DIM), w_out[i].astype(bf16), w_ple[i].astype(bf16),
                    w_ple_gate[i].astype(bf16), final_norm_g.reshape(1, d), nq)
    return out.reshape(b, s, d)
```

```python
import functools
import math

import jax
import jax.numpy as jnp
from jax import lax
from jax.experimental import pallas as pl
from jax.experimental.pallas import tpu as pltpu

D_MODEL = 1024
PLE_DIM = 256
ROPE_THETA = 10000.0
RMS_EPS = 1e-6

DIFF_WIDTH = 512
DIFF_HEADS = 4
DIFF_HD = 64
MLA_WIDTH = 512
MLA_HEADS = 8
MLA_NOPE = 64
MLA_ROPE = 32
MLA_V = 64
MLA_Q_LORA = 384
MLA_KV_LORA = 128

LANES = 128
LOG2E = math.log2(math.e)
VMEM_LIMIT = 48 * 1024 * 1024

O_DQ, O_DK, O_DG, O_MG = 0, 512, 1024, 1536
O_CQ = 2048
O_CKV = O_CQ + MLA_Q_LORA
O_KR = O_CKV + MLA_KV_LORA
D_IN_P = O_KR + LANES

_NT = (((1,), (1,)), ((), ()))
BLK = 512
HALF = BLK // 2
PAIRS = 2
AHEAD = 3


def _rms(x, g):
    return x * lax.rsqrt(jnp.mean(x * x, axis=-1, keepdims=True) + RMS_EPS) * g


def _silu(x):
    return x / (1.0 + jnp.exp(-x))


def _rope128(x, c, s, below, half):
    partner = jnp.where(below, pltpu.roll(x, half, axis=1), pltpu.roll(x, LANES - half, axis=1))
    return x * c + partner * s


def _folded(c, n):
    return jnp.where(c < n // 2, c, n + n // 2 - 1 - c)


def _proj_kernel(x_ref, pos_ref, posprev_ref, tab_ref, ng_ref, w_ref, qg_ref, wuq_ref, kvg_ref,
                 wk_ref, wvt_ref, wdvt_ref,
                 dq_ref, dk_ref, dvt_ref, dg_ref, mq_ref, mk_ref, mvt_ref, mg_ref,
                 rowtab_ref, rope_ref, nb_ref, flag_ref):
    f32 = jnp.float32
    bf16 = jnp.bfloat16
    i = pl.program_id(0)
    row = lax.broadcasted_iota(jnp.int32, (BLK, LANES), 0).astype(f32)

    def store_tables(slot, t, c, s):
        rope_ref[slot, 2 * t] = c
        rope_ref[slot, 2 * t + 1] = s * tab_ref[3 * t + 1:3 * t + 2, :]

    def prep(slot):
        nb_ref[slot] = _rms(x_ref[...], ng_ref[...]).astype(bf16)
        pos = pos_ref[...]
        base = pos[0:1, :]
        flag_ref[slot] = jnp.sum(jnp.where(pos == base + row, 0.0, 1.0))
        for t in range(2):
            ang = base * tab_ref[3 * t:3 * t + 1, :]
            cb, sb = jnp.cos(ang), jnp.sin(ang)
            cr, sr = rowtab_ref[2 * t], rowtab_ref[2 * t + 1]
            store_tables(slot, t, cb * cr - sb * sr, sb * cr + cb * sr)

    def direct_tables(slot):
        pos = posprev_ref[...]
        for t in range(2):
            ang = pos * tab_ref[3 * t:3 * t + 1, :]
            store_tables(slot, t, jnp.cos(ang), jnp.sin(ang))

    def main(slot):
        nb = nb_ref[slot]
        cd, sd, below_d = rope_ref[slot, 0], rope_ref[slot, 1], tab_ref[2:3, :] > 0.0
        cm, sm, below_m = rope_ref[slot, 2], rope_ref[slot, 3], tab_ref[5:6, :] > 0.0

        def proj(off, width):
            return jnp.dot(nb, w_ref[:, off:off + width], preferred_element_type=f32)

        lat = proj(O_CQ, D_IN_P - O_CQ)
        q_scale = DIFF_HD ** -0.5 * LOG2E
        dq = proj(O_DQ, DIFF_WIDTH)
        dk = proj(O_DK, DIFF_WIDTH)
        cqn = _rms(lat[:, :MLA_Q_LORA], qg_ref[...]).astype(bf16)
        ckvn = _rms(lat[:, O_CKV - O_CQ:O_KR - O_CQ], kvg_ref[...]).astype(bf16)
        for c in range(DIFF_WIDTH // LANES):
            sl = slice(c * LANES, (c + 1) * LANES)
            dq_ref[0, :, sl] = (_rope128(dq[:, sl], cd, sd, below_d, DIFF_HD // 2)
                                * q_scale).astype(bf16)
            dk_ref[:, sl] = _rope128(dk[:, sl], cd, sd, below_d, DIFF_HD // 2).astype(bf16)

        mq = jnp.dot(cqn, wuq_ref[...], preferred_element_type=f32)
        m_scale = (MLA_NOPE + MLA_ROPE) ** -0.5 * LOG2E
        kn = jnp.dot(ckvn, wk_ref[...], preferred_element_type=f32)
        kr = _rope128(lat[:, O_KR - O_CQ:], cm, sm, below_m, MLA_ROPE // 2)
        for h in range(MLA_HEADS):
            sl = slice(h * LANES, (h + 1) * LANES)
            mq_ref[0, :, sl] = (_rope128(mq[:, sl], cm, sm, below_m, MLA_ROPE // 2)
                                * m_scale).astype(bf16)
            mk_ref[:, sl] = (kn[:, sl] + kr).astype(bf16)

        dg_ref[0] = _silu(proj(O_DG, DIFF_WIDTH)).astype(bf16)
        mg_ref[0] = _silu(proj(O_MG, MLA_WIDTH)).astype(bf16)
        mvt_ref[0] = lax.dot_general(wvt_ref[...], ckvn, _NT,
                                     preferred_element_type=f32).astype(bf16)
        dvt_ref[0] = lax.dot_general(wdvt_ref[...], nb, _NT,
                                     preferred_element_type=f32).astype(bf16)

    def step(cur):
        prv = 1 - cur

        @pl.when(flag_ref[prv] != 0.0)
        def _():
            direct_tables(prv)

        main(prv)
        prep(cur)

    @pl.when(i == 0)
    def _():
        for t in range(2):
            ang = row * tab_ref[3 * t:3 * t + 1, :]
            rowtab_ref[2 * t] = jnp.cos(ang)
            rowtab_ref[2 * t + 1] = jnp.sin(ang)
        prep(0)

    @pl.when(i % 2 == 1)
    def _():
        step(1)

    @pl.when((i > 0) & (i % 2 == 0))
    def _():
        step(0)


def _proj_call(x2, posb, tab, ng, w_in_p, qg, wuq_p, kvg, wk_p, wvt_p, wdvt, nq):
    t = x2.shape[0]
    n = t // BLK
    bf16 = jnp.bfloat16
    nxt = lambda i: jnp.minimum(i, n - 1)
    cur = lambda i: jnp.maximum(i - 1, 0)
    row_in = lambda w, blk: pl.BlockSpec((BLK, w), lambda i: (blk(i), 0))
    full = lambda a: pl.BlockSpec(a.shape, lambda i: (0, 0))
    row = lambda w: pl.BlockSpec((BLK, w), lambda i: (cur(i), 0))
    blk3 = lambda w: pl.BlockSpec((1, w, BLK), lambda i: (cur(i), 0, 0))
    fold = lambda w: pl.BlockSpec(
        (1, BLK, w), lambda i: (cur(i) // nq * nq + _folded(cur(i) % nq, nq), 0, 0))
    rows = lambda w: jax.ShapeDtypeStruct((t, w), bf16)
    blks = lambda w: jax.ShapeDtypeStruct((n, w, BLK), bf16)
    folds = lambda w: jax.ShapeDtypeStruct((n, BLK, w), bf16)
    return pl.pallas_call(
        _proj_kernel,
        grid=(n + 1,),
        in_specs=[row_in(D_MODEL, nxt), row_in(LANES, nxt), row_in(LANES, cur), full(tab),
                  full(ng), full(w_in_p), full(qg), full(wuq_p), full(kvg), full(wk_p),
                  full(wvt_p), full(wdvt)],
        out_specs=[fold(DIFF_WIDTH), row(DIFF_WIDTH), blk3(DIFF_WIDTH), fold(DIFF_WIDTH),
                   fold(MLA_HEADS * LANES), row(MLA_HEADS * LANES), blk3(MLA_WIDTH),
                   fold(MLA_WIDTH)],
        out_shape=[folds(DIFF_WIDTH), rows(DIFF_WIDTH), blks(DIFF_WIDTH), folds(DIFF_WIDTH),
                   folds(MLA_HEADS * LANES), rows(MLA_HEADS * LANES), blks(MLA_WIDTH),
                   folds(MLA_WIDTH)],
        scratch_shapes=[pltpu.VMEM((4, BLK, LANES), jnp.float32),
                        pltpu.VMEM((2, 4, BLK, LANES), jnp.float32),
                        pltpu.VMEM((2, BLK, D_MODEL), bf16),
                        pltpu.SMEM((2,), jnp.float32)],
        compiler_params=pltpu.CompilerParams(
            dimension_semantics=("arbitrary",), vmem_limit_bytes=VMEM_LIMIT),
        name="proj",
    )(x2, posb, posb, tab, ng, w_in_p, qg, wuq_p, kvg, wk_p, wvt_p, wdvt)


SUM_LO, SUM_HI = 2.0 ** -64, 2.0 ** 64
OUT_LIMIT = 2.0 ** 100


def _attn_fast(nq, first, get_q, get_k, vt_ref, acc_ref, l_ref, epilogue):
    f32 = jnp.float32

    def diagonal(side, j):
        out = []
        for i in range(2):
            out += [(side, j, i, 0, HALF, 0, HALF, True),
                    (side, j, i, 0, HALF, HALF, HALF, False),
                    (side, j, i, HALF, HALF, HALF, HALF, True)]
        return out

    def full(side, j):
        return [(side, j, i, 0, BLK, q0, HALF, False) for i in range(2) for q0 in (0, HALF)]

    units, done = [], {}
    for p in range(PAIRS):
        for s, blocks in ((0, first + p), (1, nq - 1 - first - p)):
            for j in range(blocks):
                units += full((p, s), j)
            units += diagonal((p, s), blocks)
            done[len(units)] = (p, s)
    started = set()

    def qk(u):
        side, j, i, k0, nk, q0, nq_ = units[u][:7]
        return lax.dot_general(get_k(i, j, k0, nk), get_q(side, i, q0, nq_), _NT,
                               preferred_element_type=f32)

    bad = jnp.zeros((1, BLK), f32)
    scores = {u: qk(u) for u in range(AHEAD)}
    for u, (side, j, i, k0, nk, q0, nq_, masked) in enumerate(units):
        s = scores.pop(u)
        if masked:
            krow = lax.broadcasted_iota(jnp.int32, s.shape, 0) + k0
            qcol = lax.broadcasted_iota(jnp.int32, s.shape, 1) + q0
            s = jnp.where(krow <= qcol, s, -jnp.inf)
        p = jnp.exp2(s)
        col = jnp.sum(p, axis=0, keepdims=True)
        pb = p.astype(jnp.bfloat16)
        if u + AHEAD < len(units):
            scores[u + AHEAD] = qk(u + AHEAD)
        pv = jnp.dot(vt_ref[0, j, :, k0:k0 + nk], pb, preferred_element_type=f32)
        where = side + (i, slice(None), slice(q0, q0 + nq_))
        if (side, i, q0) in started:
            l_ref[where] += col
            acc_ref[where] += pv
        else:
            started.add((side, i, q0))
            l_ref[where] = col
            acc_ref[where] = pv
        if u + 1 in done:
            bad = bad + epilogue(done[u + 1])
    return bad


def _sides():
    return [(p, s) for p in range(PAIRS) for s in range(2)]


def _attn_rescaling(nq, get_q, get_k, vt_ref, acc_ref, l_ref):
    f32 = jnp.float32
    first = pl.program_id(2) * PAIRS
    acc_ref[...] = jnp.zeros(acc_ref.shape, f32)

    for side in _sides():
        n_full = first + side[0] if side[1] == 0 else nq - 1 - first - side[0]

        def step(j, carry, masked, side=side):
            vt = vt_ref[0, j]
            out = []
            for i in range(2):
                m_prev, l_prev = carry[2 * i], carry[2 * i + 1]
                s = lax.dot_general(get_k(i, j, 0, BLK), get_q(side, i, 0, BLK), _NT,
                                    preferred_element_type=f32)
                if masked:
                    krow = lax.broadcasted_iota(jnp.int32, s.shape, 0)
                    qcol = lax.broadcasted_iota(jnp.int32, s.shape, 1)
                    s = jnp.where(krow <= qcol, s, -jnp.inf)
                m_new = jnp.maximum(m_prev, jnp.max(s, axis=0, keepdims=True))
                alpha = jnp.exp2(m_prev - m_new)
                p = jnp.exp2(s - m_new)
                l_new = alpha * l_prev + jnp.sum(p, axis=0, keepdims=True)
                acc_ref[side + (i,)] = alpha * acc_ref[side + (i,)] + jnp.dot(
                    vt, p.astype(jnp.bfloat16), preferred_element_type=f32)
                out += [m_new, l_new]
            return tuple(out)

        neg = jnp.full((1, BLK), -jnp.inf, f32)
        zero = jnp.zeros((1, BLK), f32)
        carry = lax.fori_loop(0, n_full, lambda j, cr: step(j, cr, False),
                              (neg, zero, neg, zero))
        carry = step(n_full, carry, True)
        l_ref[side + (0,)] = carry[1]
        l_ref[side + (1,)] = carry[3]


def _sums_out_of_range(l0, l1):
    ok = ((l0 > SUM_LO) & (l0 < SUM_HI)) & ((l1 > SUM_LO) & (l1 < SUM_HI))
    return jnp.where(ok, 0.0, 1.0)


def _guarded(nq, get_q, get_k, vt_ref, acc_ref, l_ref, bad_ref, epilogue):
    for step in range(nq // 2 // PAIRS):
        @pl.when(pl.program_id(2) == step)
        def _(step=step):
            bad_ref[0] = jnp.sum(_attn_fast(nq, step * PAIRS, get_q, get_k, vt_ref, acc_ref,
                                            l_ref, epilogue))

    @pl.when(bad_ref[0] > 0.0)
    def _():
        _attn_rescaling(nq, get_q, get_k, vt_ref, acc_ref, l_ref)
        for side in _sides():
            epilogue(side)


def _key_rows(j, k0, n):
    if isinstance(j, int):
        return slice(j * BLK + k0, j * BLK + k0 + n)
    return pl.ds(pl.multiple_of(j * BLK + k0, HALF), n)


def _diff_kernel(nq, lam_ref, q_ref, k_ref, vt_ref, g_ref, sg_ref, o_ref,
                 qz_ref, acc_ref, l_ref, bad_ref):
    f32 = jnp.float32
    for p, s in _sides():
        q = q_ref[0, s, p]
        lane = lax.broadcasted_iota(jnp.int32, q.shape, 1)
        zero = jnp.zeros_like(q)
        qz_ref[p, s, 0] = jnp.where(lane < DIFF_HD, q, zero)
        qz_ref[p, s, 1] = jnp.where(lane >= DIFF_HD, q, zero)

    lam_rows = lam_ref[...]
    lam_init = 0.8 - 0.6 * math.exp(-0.3 * 0)
    lam = (jnp.exp(jnp.sum(lam_rows[0:1] * lam_rows[1:2], axis=1, keepdims=True))
           - jnp.exp(jnp.sum(lam_rows[2:3] * lam_rows[3:4], axis=1, keepdims=True))
           + lam_init)

    def epilogue(side):
        p, s = side
        l0, l1 = l_ref[p, s, 0], l_ref[p, s, 1]
        ot = acc_ref[p, s, 0] * (1.0 / l0) - acc_ref[p, s, 1] * (lam / l1)
        ms = jnp.mean(ot * ot, axis=0, keepdims=True)
        ot = ot * lax.rsqrt(ms + RMS_EPS)
        o = ot.T * (sg_ref[...] * (1.0 - lam_init)) * g_ref[0, s, p].astype(f32)
        o_ref[0, s, p] = o.astype(o_ref.dtype)
        return _sums_out_of_range(l0, l1) + jnp.where(ms < OUT_LIMIT, 0.0, 1.0)

    _guarded(nq,
             lambda side, i, q0, n: qz_ref[side[0], side[1], i, q0:q0 + n, :],
             lambda i, j, k0, n: k_ref[0, _key_rows(j, k0, n), :],
             vt_ref, acc_ref, l_ref, bad_ref, epilogue)


def _mla_kernel(nq, q_ref, k_ref, vt_ref, g_ref, o_ref, acc_ref, l_ref, bad_ref):
    f32 = jnp.float32

    def epilogue(side):
        p, s = side
        l0, l1 = l_ref[p, s, 0], l_ref[p, s, 1]
        feat = lax.broadcasted_iota(jnp.int32, (LANES, BLK), 0)
        ot = jnp.where(feat < MLA_V, acc_ref[p, s, 0] * (1.0 / l0), acc_ref[p, s, 1] * (1.0 / l1))
        peak = jnp.max(jnp.abs(ot), axis=0, keepdims=True)
        o_ref[0, s, p] = (ot.T * g_ref[0, s, p].astype(f32)).astype(o_ref.dtype)
        return _sums_out_of_range(l0, l1) + jnp.where(peak < OUT_LIMIT, 0.0, 1.0)

    _guarded(nq,
             lambda side, i, q0, n: q_ref[0, side[1], side[0], q0:q0 + n,
                                          i * LANES:(i + 1) * LANES],
             lambda i, j, k0, n: k_ref[0, _key_rows(j, k0, n), i * LANES:(i + 1) * LANES],
             vt_ref, acc_ref, l_ref, bad_ref, epilogue)


def _attn_scratch():
    return [pltpu.VMEM((PAIRS, 2, 2, LANES, BLK), jnp.float32),
            pltpu.VMEM((PAIRS, 2, 2, 1, BLK), jnp.float32),
            pltpu.SMEM((1,), jnp.float32)]


def _pair_spec(width):
    return pl.BlockSpec((1, 2, PAIRS, BLK, width), lambda bi, h, c: (bi, 0, c, 0, h))


def _diff_call(lam_rows, dq, dk, dvt, dg, subg):
    b, _, npair, _, _ = dq.shape
    nq = 2 * npair
    s = nq * BLK
    kspec = pl.BlockSpec((1, s, LANES), lambda bi, h, c: (bi, 0, h))
    vtspec = pl.BlockSpec((1, nq, LANES, BLK), lambda bi, h, c: (bi, 0, h, 0))
    return pl.pallas_call(
        functools.partial(_diff_kernel, nq),
        grid=(b, DIFF_HEADS, npair // PAIRS),
        in_specs=[pl.BlockSpec(lam_rows.shape, lambda bi, h, c: (0, 0)),
                  _pair_spec(LANES), kspec, vtspec, _pair_spec(LANES),
                  pl.BlockSpec(subg.shape, lambda bi, h, c: (0, 0))],
        out_specs=_pair_spec(LANES),
        out_shape=jax.ShapeDtypeStruct(dq.shape, jnp.bfloat16),
        scratch_shapes=[pltpu.VMEM((PAIRS, 2, 2, BLK, LANES), jnp.bfloat16)] + _attn_scratch(),
        compiler_params=pltpu.CompilerParams(
            dimension_semantics=("arbitrary",) * 3, vmem_limit_bytes=VMEM_LIMIT),
        name="diffattn",
    )(lam_rows, dq, dk, dvt, dg, subg)


def _mla_call(mq, mk, mvt, mg):
    b, _, npair, _, _ = mq.shape
    nq = 2 * npair
    s = nq * BLK
    return pl.pallas_call(
        functools.partial(_mla_kernel, nq),
        grid=(b, MLA_HEADS // 2, npair // PAIRS),
        in_specs=[_pair_spec(2 * LANES),
                  pl.BlockSpec((1, s, 2 * LANES), lambda bi, h, c: (bi, 0, h)),
                  pl.BlockSpec((1, nq, LANES, BLK), lambda bi, h, c: (bi, 0, h, 0)),
                  _pair_spec(LANES)],
        out_specs=_pair_spec(LANES),
        out_shape=jax.ShapeDtypeStruct(mg.shape, jnp.bfloat16),
        scratch_shapes=_attn_scratch(),
        compiler_params=pltpu.CompilerParams(
            dimension_semantics=("arbitrary",) * 3, vmem_limit_bytes=VMEM_LIMIT),
        name="mlaattn",
    )(mq, mk, mvt, mg)


def _out_kernel(x_ref, od_ref, om_ref, p_ref, wo_ref, wp_ref, wg_ref, fg_ref, o_ref):
    f32 = jnp.float32
    bf16 = jnp.bfloat16
    for r in (slice(0, HALF), slice(HALF, BLK)):
        h = (x_ref[r, :]
             + jnp.dot(od_ref[0, r, :], wo_ref[0:DIFF_WIDTH, :], preferred_element_type=f32)
             + jnp.dot(om_ref[0, r, :], wo_ref[DIFF_WIDTH:, :], preferred_element_type=f32))
        gate = jax.nn.sigmoid(jnp.dot(h.astype(bf16), wg_ref[...], preferred_element_type=f32))
        emb = jnp.dot(p_ref[r, :].astype(bf16), wp_ref[...], preferred_element_type=f32)
        h = h + emb * gate
        o_ref[r, :] = _rms(h, fg_ref[...])


def _out_call(x2, od, om, p2, wo, wp, wg, fg, nq):
    t = x2.shape[0]
    row = lambda w: pl.BlockSpec((BLK, w), lambda i: (i, 0))
    full = lambda a: pl.BlockSpec(a.shape, lambda i: (0, 0))
    fold = lambda w: pl.BlockSpec((1, BLK, w), lambda i: (i // nq * nq + _folded(i % nq, nq), 0, 0))
    return pl.pallas_call(
        _out_kernel,
        grid=(t // BLK,),
        in_specs=[row(D_MODEL), fold(DIFF_WIDTH), fold(MLA_WIDTH), row(PLE_DIM),
                  full(wo), full(wp), full(wg), full(fg)],
        out_specs=row(D_MODEL),
        out_shape=jax.ShapeDtypeStruct((t, D_MODEL), jnp.float32),
        compiler_params=pltpu.CompilerParams(
            dimension_semantics=("arbitrary",), vmem_limit_bytes=VMEM_LIMIT),
        name="outproj",
    )(x2, od, om, p2, wo, wp, wg, fg)


def _rope_tables():
    f32 = jnp.float32
    lane = jnp.arange(LANES)
    inv_d = ROPE_THETA ** (-jnp.arange(0, DIFF_HD, 2, dtype=f32) / DIFF_HD)
    half_d = DIFF_HD // 2
    inv_d_l = inv_d[lane % half_d]
    sp_d = jnp.where(lane % DIFF_HD >= half_d, 1.0, 0.0)
    sm_d = jnp.where(lane % DIFF_HD < half_d, -1.0, 0.0)
    inv_m = ROPE_THETA ** (-jnp.arange(0, MLA_ROPE, 2, dtype=f32) / MLA_ROPE)
    half_m = MLA_ROPE // 2
    in_rope = (lane >= MLA_NOPE) & (lane < MLA_NOPE + MLA_ROPE)
    inv_m_l = jnp.where(in_rope, inv_m[(lane - MLA_NOPE) % half_m], 0.0)
    sp_m = jnp.where(in_rope & (lane >= MLA_NOPE + half_m), 1.0, 0.0)
    sm_m = jnp.where(in_rope & (lane < MLA_NOPE + half_m), -1.0, 0.0)
    zero = jnp.zeros((LANES,), f32)
    return jnp.stack([inv_d_l, sp_d + sm_d, sp_d, inv_m_l, sp_m + sm_m, sp_m, zero, zero]).astype(f32)


def kernel(x, p, positions, norm_g, w_in, diff_lambda, diff_subln_g, mla_q_norm_g, w_uq,
           mla_kv_norm_g, w_ukv, w_out, w_ple, w_ple_gate, final_norm_g):
    b, s, d = x.shape
    t = b * s
    nq = s // BLK
    assert s % (2 * PAIRS * BLK) == 0
    f32 = jnp.float32
    bf16 = jnp.bfloat16
    i = 0

    offs = [0, 512, 1024, 1536, 2048, 2432, 2560, 2592, 3104]
    seg = [w_in[i][:, offs[j]:offs[j + 1]] for j in range(8)]
    wdq, wdk, wdv, wdg, wcq, wckv, wkr, wmg = seg
    zc = lambda n: jnp.zeros((d, n), f32)
    w_in_p = jnp.concatenate(
        [wdq, wdk, wdg, wmg, wcq, wckv, zc(MLA_NOPE), wkr, zc(LANES - MLA_NOPE - MLA_ROPE)],
        axis=1).astype(bf16)
    wuq_p = jnp.pad(w_uq[i].reshape(MLA_Q_LORA, MLA_HEADS, MLA_NOPE + MLA_ROPE),
                    ((0, 0), (0, 0), (0, LANES - MLA_NOPE - MLA_ROPE))
                    ).reshape(MLA_Q_LORA, MLA_HEADS * LANES).astype(bf16)
    wkv3 = w_ukv[i].reshape(MLA_KV_LORA, MLA_HEADS, MLA_NOPE + MLA_V)
    wk_p = jnp.pad(wkv3[:, :, :MLA_NOPE], ((0, 0), (0, 0), (0, LANES - MLA_NOPE))
                   ).reshape(MLA_KV_LORA, MLA_HEADS * LANES).astype(bf16)
    wvt_p = wkv3[:, :, MLA_NOPE:].reshape(MLA_KV_LORA, MLA_WIDTH).T.astype(bf16)
    wdvt = wdv.T.astype(bf16)

    x2 = x.reshape(t, d)
    posb = jnp.broadcast_to(positions.astype(f32).reshape(t, 1), (t, LANES))
    dq, dk, dvt, dg, mq, mk, mvt, mg = _proj_call(
        x2, posb, _rope_tables(), norm_g[i].reshape(1, d), w_in_p,
        mla_q_norm_g[i].reshape(1, MLA_Q_LORA), wuq_p,
        mla_kv_norm_g[i].reshape(1, MLA_KV_LORA), wk_p, wvt_p, wdvt, nq)

    r3 = lambda a: a.reshape(b, s, a.shape[-1])
    r4 = lambda a: a.reshape(b, nq, a.shape[-2], BLK)
    r5 = lambda a: a.reshape(b, 2, nq // 2, BLK, a.shape[-1])
    od = _diff_call(diff_lambda[i].astype(f32), r5(dq), r3(dk), r4(dvt), r5(dg),
                    diff_subln_g[i].reshape(1, 2 * DIFF_HD))
    om = _mla_call(r5(mq), r3(mk), r4(mvt), r5(mg))

    out = _out_call(x2, od.reshape(t // BLK, BLK, DIFF_WIDTH), om.reshape(t // BLK, BLK, MLA_WIDTH),
                    p[i].reshape(t, PLE_DIM), w_out[i].astype(bf16), w_ple[i].astype(bf16),
                    w_ple_gate[i].astype(bf16), final_norm_g.reshape(1, d), nq)
    return out.reshape(b, s, d)
```

```python
import functools
import math

import jax
import jax.numpy as jnp
from jax import lax
from jax.experimental import pallas as pl
from jax.experimental.pallas import tpu as pltpu

D_MODEL = 1024
PLE_DIM = 256
ROPE_THETA = 10000.0
RMS_EPS = 1e-6

DIFF_WIDTH = 512
DIFF_HEADS = 4
DIFF_HD = 64
MLA_WIDTH = 512
MLA_HEADS = 8
MLA_NOPE = 64
MLA_ROPE = 32
MLA_V = 64
MLA_Q_LORA = 384
MLA_KV_LORA = 128

LANES = 128
LOG2E = math.log2(math.e)
VMEM_LIMIT = 48 * 1024 * 1024

O_DQ, O_DK, O_DG, O_MG = 0, 512, 1024, 1536
O_CQ = 2048
O_CKV = O_CQ + MLA_Q_LORA
O_KR = O_CKV + MLA_KV_LORA
D_IN_P = O_KR + LANES

_NT = (((1,), (1,)), ((), ()))
BLK = 512
HALF = BLK // 2
PAIRS = 2
AHEAD = 3


def _rms(x, g):
    return x * lax.rsqrt(jnp.mean(x * x, axis=-1, keepdims=True) + RMS_EPS) * g


def _silu(x):
    return x / (1.0 + jnp.exp(-x))


def _rope128(x, c, s, below, half):
    partner = jnp.where(below, pltpu.roll(x, half, axis=1), pltpu.roll(x, LANES - half, axis=1))
    return x * c + partner * s


def _folded(c, n):
    return jnp.where(c < n // 2, c, n + n // 2 - 1 - c)


def _proj_kernel(x_ref, pos_ref, posprev_ref, tab_ref, ng_ref, w_ref, qg_ref, wuq_ref, kvg_ref,
                 wk_ref, wvt_ref, wdvt_ref,
                 dq_ref, dk_ref, dvt_ref, dg_ref, mq_ref, mk_ref, mvt_ref, mg_ref,
                 rowtab_ref, rope_ref, nb_ref, flag_ref):
    f32 = jnp.float32
    bf16 = jnp.bfloat16
    i = pl.program_id(0)
    row = lax.broadcasted_iota(jnp.int32, (BLK, LANES), 0).astype(f32)

    def store_tables(slot, t, c, s):
        rope_ref[slot, 2 * t] = c
        rope_ref[slot, 2 * t + 1] = s * tab_ref[3 * t + 1:3 * t + 2, :]

    def prep(slot):
        nb_ref[slot] = _rms(x_ref[...], ng_ref[...]).astype(bf16)
        pos = pos_ref[0]
        base = pos[:, 0:1]
        idx = lax.broadcasted_iota(jnp.int32, (1, BLK), 1).astype(f32)
        flag_ref[slot] = jnp.sum(jnp.where(pos == base + idx, 0.0, 1.0))
        for t in range(2):
            ang = base * tab_ref[3 * t:3 * t + 1, :]
            cb, sb = jnp.cos(ang), jnp.sin(ang)
            cr, sr = rowtab_ref[2 * t], rowtab_ref[2 * t + 1]
            store_tables(slot, t, cb * cr - sb * sr, sb * cr + cb * sr)

    def direct_tables(slot):
        pos = jnp.broadcast_to(posprev_ref[0], (LANES, BLK)).T
        for t in range(2):
            ang = pos * tab_ref[3 * t:3 * t + 1, :]
            store_tables(slot, t, jnp.cos(ang), jnp.sin(ang))

    def main(slot):
        nb = nb_ref[slot]
        cd, sd, below_d = rope_ref[slot, 0], rope_ref[slot, 1], tab_ref[2:3, :] > 0.0
        cm, sm, below_m = rope_ref[slot, 2], rope_ref[slot, 3], tab_ref[5:6, :] > 0.0

        def proj(off, width):
            return jnp.dot(nb, w_ref[:, off:off + width], preferred_element_type=f32)

        lat = proj(O_CQ, D_IN_P - O_CQ)
        q_scale = DIFF_HD ** -0.5 * LOG2E
        dq = proj(O_DQ, DIFF_WIDTH)
        dk = proj(O_DK, DIFF_WIDTH)
        cqn = _rms(lat[:, :MLA_Q_LORA], qg_ref[...]).astype(bf16)
        ckvn = _rms(lat[:, O_CKV - O_CQ:O_KR - O_CQ], kvg_ref[...]).astype(bf16)
        for c in range(DIFF_WIDTH // LANES):
            sl = slice(c * LANES, (c + 1) * LANES)
            dq_ref[0, :, sl] = (_rope128(dq[:, sl], cd, sd, below_d, DIFF_HD // 2)
                                * q_scale).astype(bf16)
            dk_ref[:, sl] = _rope128(dk[:, sl], cd, sd, below_d, DIFF_HD // 2).astype(bf16)

        mq = jnp.dot(cqn, wuq_ref[...], preferred_element_type=f32)
        m_scale = (MLA_NOPE + MLA_ROPE) ** -0.5 * LOG2E
        kn = jnp.dot(ckvn, wk_ref[...], preferred_element_type=f32)
        kr = _rope128(lat[:, O_KR - O_CQ:], cm, sm, below_m, MLA_ROPE // 2)
        for h in range(MLA_HEADS):
            sl = slice(h * LANES, (h + 1) * LANES)
            mq_ref[0, :, sl] = (_rope128(mq[:, sl], cm, sm, below_m, MLA_ROPE // 2)
                                * m_scale).astype(bf16)
            mk_ref[:, sl] = (kn[:, sl] + kr).astype(bf16)

        dg_ref[0] = _silu(proj(O_DG, DIFF_WIDTH)).astype(bf16)
        mg_ref[0] = _silu(proj(O_MG, MLA_WIDTH)).astype(bf16)
        mvt_ref[0] = lax.dot_general(wvt_ref[...], ckvn, _NT,
                                     preferred_element_type=f32).astype(bf16)
        dvt_ref[0] = lax.dot_general(wdvt_ref[...], nb, _NT,
                                     preferred_element_type=f32).astype(bf16)

    def step(cur):
        prv = 1 - cur

        @pl.when(flag_ref[prv] != 0.0)
        def _():
            direct_tables(prv)

        main(prv)
        prep(cur)

    @pl.when(i == 0)
    def _():
        for t in range(2):
            ang = row * tab_ref[3 * t:3 * t + 1, :]
            rowtab_ref[2 * t] = jnp.cos(ang)
            rowtab_ref[2 * t + 1] = jnp.sin(ang)
        prep(0)

    @pl.when(i % 2 == 1)
    def _():
        step(1)

    @pl.when((i > 0) & (i % 2 == 0))
    def _():
        step(0)


def _proj_call(x2, posb, tab, ng, w_in_p, qg, wuq_p, kvg, wk_p, wvt_p, wdvt, nq):
    t = x2.shape[0]
    n = t // BLK
    bf16 = jnp.bfloat16
    nxt = lambda i: jnp.minimum(i, n - 1)
    cur = lambda i: jnp.maximum(i - 1, 0)
    row_in = lambda w, blk: pl.BlockSpec((BLK, w), lambda i: (blk(i), 0))
    pos_in = lambda blk: pl.BlockSpec((1, 1, BLK), lambda i: (blk(i), 0, 0))
    full = lambda a: pl.BlockSpec(a.shape, lambda i: (0, 0))
    row = lambda w: pl.BlockSpec((BLK, w), lambda i: (cur(i), 0))
    blk3 = lambda w: pl.BlockSpec((1, w, BLK), lambda i: (cur(i), 0, 0))
    fold = lambda w: pl.BlockSpec(
        (1, BLK, w), lambda i: (cur(i) // nq * nq + _folded(cur(i) % nq, nq), 0, 0))
    rows = lambda w: jax.ShapeDtypeStruct((t, w), bf16)
    blks = lambda w: jax.ShapeDtypeStruct((n, w, BLK), bf16)
    folds = lambda w: jax.ShapeDtypeStruct((n, BLK, w), bf16)
    return pl.pallas_call(
        _proj_kernel,
        grid=(n + 1,),
        in_specs=[row_in(D_MODEL, nxt), pos_in(nxt), pos_in(cur), full(tab),
                  full(ng), full(w_in_p), full(qg), full(wuq_p), full(kvg), full(wk_p),
                  full(wvt_p), full(wdvt)],
        out_specs=[fold(DIFF_WIDTH), row(DIFF_WIDTH), blk3(DIFF_WIDTH), fold(DIFF_WIDTH),
                   fold(MLA_HEADS * LANES), row(MLA_HEADS * LANES), blk3(MLA_WIDTH),
                   fold(MLA_WIDTH)],
        out_shape=[folds(DIFF_WIDTH), rows(DIFF_WIDTH), blks(DIFF_WIDTH), folds(DIFF_WIDTH),
                   folds(MLA_HEADS * LANES), rows(MLA_HEADS * LANES), blks(MLA_WIDTH),
                   folds(MLA_WIDTH)],
        scratch_shapes=[pltpu.VMEM((4, BLK, LANES), jnp.float32),
                        pltpu.VMEM((2, 4, BLK, LANES), jnp.float32),
                        pltpu.VMEM((2, BLK, D_MODEL), bf16),
                        pltpu.SMEM((2,), jnp.float32)],
        compiler_params=pltpu.CompilerParams(
            dimension_semantics=("arbitrary",), vmem_limit_bytes=VMEM_LIMIT),
        name="proj",
    )(x2, posb, posb, tab, ng, w_in_p, qg, wuq_p, kvg, wk_p, wvt_p, wdvt)


SUM_LO, SUM_HI = 2.0 ** -64, 2.0 ** 64
OUT_LIMIT = 2.0 ** 100


def _attn_fast(nq, get_q, get_k, vt_ref, acc_ref, l_ref, epilogue):
    f32 = jnp.float32
    first = pl.program_id(2) * PAIRS

    _zero(acc_ref, l_ref)

    def diagonal(side, j):
        out = []
        for i in range(2):
            out += [(side, j, i, 0, HALF, 0, HALF, True),
                    (side, j, i, 0, HALF, HALF, HALF, False),
                    (side, j, i, HALF, HALF, HALF, HALF, True)]
        return out

    def full(side, j):
        return [(side, j, i, 0, BLK, q0, HALF, False) for i in range(2) for q0 in (0, HALF)]

    units, done = [], {}
    for p in range(PAIRS):
        c = first + p
        for t in range(nq // 2 - 1):
            on_side0 = t < c
            units += full((p, jnp.where(on_side0, 0, 1)), jnp.where(on_side0, t, t - c))
        units += diagonal((p, 0), c)
        done[len(units)] = (p, 0)
        units += diagonal((p, 1), nq - 1 - c)
        for t in range(nq // 2 - 1, nq - 1):
            units += full((p, 1), t - c)
        done[len(units)] = (p, 1)

    def qk(u):
        side, j, i, k0, nk, q0, nq_ = units[u][:7]
        return lax.dot_general(get_k(i, j, k0, nk), get_q(side, i, q0, nq_), _NT,
                               preferred_element_type=f32)

    bad = jnp.zeros((1, BLK), f32)
    scores = {u: qk(u) for u in range(AHEAD)}
    for u, (side, j, i, k0, nk, q0, nq_, masked) in enumerate(units):
        s = scores.pop(u)
        if masked:
            krow = lax.broadcasted_iota(jnp.int32, s.shape, 0) + k0
            qcol = lax.broadcasted_iota(jnp.int32, s.shape, 1) + q0
            s = jnp.where(krow <= qcol, s, -jnp.inf)
        p = jnp.exp2(s)
        col = jnp.sum(p, axis=0, keepdims=True)
        pb = p.astype(jnp.bfloat16)
        if u + AHEAD < len(units):
            scores[u + AHEAD] = qk(u + AHEAD)
        pv = jnp.dot(vt_ref[0, j, :, k0:k0 + nk], pb, preferred_element_type=f32)
        where = side + (i, slice(None), slice(q0, q0 + nq_))
        l_ref[where] += col
        acc_ref[where] += pv
        if u + 1 in done:
            bad = bad + epilogue(done[u + 1])
    return bad


def _sides():
    return [(p, s) for p in range(PAIRS) for s in range(2)]


def _attn_rescaling(nq, get_q, get_k, vt_ref, acc_ref, l_ref):
    f32 = jnp.float32
    first = pl.program_id(2) * PAIRS
    acc_ref[...] = jnp.zeros(acc_ref.shape, f32)

    for side in _sides():
        n_full = first + side[0] if side[1] == 0 else nq - 1 - first - side[0]

        def step(j, carry, masked, side=side):
            vt = vt_ref[0, j]
            out = []
            for i in range(2):
                m_prev, l_prev = carry[2 * i], carry[2 * i + 1]
                s = lax.dot_general(get_k(i, j, 0, BLK), get_q(side, i, 0, BLK), _NT,
                                    preferred_element_type=f32)
                if masked:
                    krow = lax.broadcasted_iota(jnp.int32, s.shape, 0)
                    qcol = lax.broadcasted_iota(jnp.int32, s.shape, 1)
                    s = jnp.where(krow <= qcol, s, -jnp.inf)
                m_new = jnp.maximum(m_prev, jnp.max(s, axis=0, keepdims=True))
                alpha = jnp.exp2(m_prev - m_new)
                p = jnp.exp2(s - m_new)
                l_new = alpha * l_prev + jnp.sum(p, axis=0, keepdims=True)
                acc_ref[side + (i,)] = alpha * acc_ref[side + (i,)] + jnp.dot(
                    vt, p.astype(jnp.bfloat16), preferred_element_type=f32)
                out += [m_new, l_new]
            return tuple(out)

        neg = jnp.full((1, BLK), -jnp.inf, f32)
        zero = jnp.zeros((1, BLK), f32)
        carry = lax.fori_loop(0, n_full, lambda j, cr: step(j, cr, False),
                              (neg, zero, neg, zero))
        carry = step(n_full, carry, True)
        l_ref[side + (0,)] = carry[1]
        l_ref[side + (1,)] = carry[3]


def _sums_out_of_range(l0, l1):
    ok = ((l0 > SUM_LO) & (l0 < SUM_HI)) & ((l1 > SUM_LO) & (l1 < SUM_HI))
    return jnp.where(ok, 0.0, 1.0)


def _guarded(nq, get_q, get_k, vt_ref, acc_ref, l_ref, epilogue):
    n_bad = jnp.sum(_attn_fast(nq, get_q, get_k, vt_ref, acc_ref, l_ref, epilogue))

    @pl.when(n_bad > 0.0)
    def _():
        _attn_rescaling(nq, get_q, get_k, vt_ref, acc_ref, l_ref)
        for side in _sides():
            epilogue(side)


def _zero(*refs):
    for ref in refs:
        ref[...] = jnp.zeros(ref.shape, ref.dtype)


def _key_rows(j, k0, n):
    return pl.ds(pl.multiple_of(j * BLK + k0, HALF), n)


def _diff_kernel(nq, lam_ref, q_ref, k_ref, vt_ref, g_ref, sg_ref, o_ref, qz_ref, acc_ref, l_ref):
    f32 = jnp.float32
    for p, s in _sides():
        q = q_ref[0, s, p]
        lane = lax.broadcasted_iota(jnp.int32, q.shape, 1)
        zero = jnp.zeros_like(q)
        qz_ref[p, s, 0] = jnp.where(lane < DIFF_HD, q, zero)
        qz_ref[p, s, 1] = jnp.where(lane >= DIFF_HD, q, zero)

    lam_rows = lam_ref[...]
    lam_init = 0.8 - 0.6 * math.exp(-0.3 * 0)
    lam = (jnp.exp(jnp.sum(lam_rows[0:1] * lam_rows[1:2], axis=1, keepdims=True))
           - jnp.exp(jnp.sum(lam_rows[2:3] * lam_rows[3:4], axis=1, keepdims=True))
           + lam_init)

    def epilogue(side):
        p, s = side
        l0, l1 = l_ref[p, s, 0], l_ref[p, s, 1]
        ot = acc_ref[p, s, 0] * (1.0 / l0) - acc_ref[p, s, 1] * (lam / l1)
        ms = jnp.mean(ot * ot, axis=0, keepdims=True)
        ot = ot * lax.rsqrt(ms + RMS_EPS)
        o = ot.T * (sg_ref[...] * (1.0 - lam_init)) * g_ref[0, s, p].astype(f32)
        o_ref[0, s, p] = o.astype(o_ref.dtype)
        return _sums_out_of_range(l0, l1) + jnp.where(ms < OUT_LIMIT, 0.0, 1.0)

    _guarded(nq,
             lambda side, i, q0, n: qz_ref[side[0], side[1], i, q0:q0 + n, :],
             lambda i, j, k0, n: k_ref[0, _key_rows(j, k0, n), :],
             vt_ref, acc_ref, l_ref, epilogue)


def _mla_kernel(nq, q_ref, k_ref, vt_ref, g_ref, o_ref, acc_ref, l_ref):
    f32 = jnp.float32

    def epilogue(side):
        p, s = side
        l0, l1 = l_ref[p, s, 0], l_ref[p, s, 1]
        feat = lax.broadcasted_iota(jnp.int32, (LANES, BLK), 0)
        ot = jnp.where(feat < MLA_V, acc_ref[p, s, 0] * (1.0 / l0), acc_ref[p, s, 1] * (1.0 / l1))
        peak = jnp.max(jnp.abs(ot), axis=0, keepdims=True)
        o_ref[0, s, p] = (ot.T * g_ref[0, s, p].astype(f32)).astype(o_ref.dtype)
        return _sums_out_of_range(l0, l1) + jnp.where(peak < OUT_LIMIT, 0.0, 1.0)

    _guarded(nq,
             lambda side, i, q0, n: q_ref[0, side[1], side[0], q0:q0 + n,
                                          i * LANES:(i + 1) * LANES],
             lambda i, j, k0, n: k_ref[0, _key_rows(j, k0, n), i * LANES:(i + 1) * LANES],
             vt_ref, acc_ref, l_ref, epilogue)


def _attn_scratch():
    return [pltpu.VMEM((PAIRS, 2, 2, LANES, BLK), jnp.float32),
            pltpu.VMEM((PAIRS, 2, 2, 1, BLK), jnp.float32)]


def _pair_spec(width):
    return pl.BlockSpec((1, 2, PAIRS, BLK, width), lambda bi, h, c: (bi, 0, c, 0, h))


def _diff_call(lam_rows, dq, dk, dvt, dg, subg):
    b, _, npair, _, _ = dq.shape
    nq = 2 * npair
    s = nq * BLK
    kspec = pl.BlockSpec((1, s, LANES), lambda bi, h, c: (bi, 0, h))
    vtspec = pl.BlockSpec((1, nq, LANES, BLK), lambda bi, h, c: (bi, 0, h, 0))
    return pl.pallas_call(
        functools.partial(_diff_kernel, nq),
        grid=(b, DIFF_HEADS, npair // PAIRS),
        in_specs=[pl.BlockSpec(lam_rows.shape, lambda bi, h, c: (0, 0)),
                  _pair_spec(LANES), kspec, vtspec, _pair_spec(LANES),
                  pl.BlockSpec(subg.shape, lambda bi, h, c: (0, 0))],
        out_specs=_pair_spec(LANES),
        out_shape=jax.ShapeDtypeStruct(dq.shape, jnp.bfloat16),
        scratch_shapes=[pltpu.VMEM((PAIRS, 2, 2, BLK, LANES), jnp.bfloat16)] + _attn_scratch(),
        compiler_params=pltpu.CompilerParams(
            dimension_semantics=("arbitrary",) * 3, vmem_limit_bytes=VMEM_LIMIT),
        name="diffattn",
    )(lam_rows, dq, dk, dvt, dg, subg)


def _mla_call(mq, mk, mvt, mg):
    b, _, npair, _, _ = mq.shape
    nq = 2 * npair
    s = nq * BLK
    return pl.pallas_call(
        functools.partial(_mla_kernel, nq),
        grid=(b, MLA_HEADS // 2, npair // PAIRS),
        in_specs=[_pair_spec(2 * LANES),
                  pl.BlockSpec((1, s, 2 * LANES), lambda bi, h, c: (bi, 0, h)),
                  pl.BlockSpec((1, nq, LANES, BLK), lambda bi, h, c: (bi, 0, h, 0)),
                  _pair_spec(LANES)],
        out_specs=_pair_spec(LANES),
        out_shape=jax.ShapeDtypeStruct(mg.shape, jnp.bfloat16),
        scratch_shapes=_attn_scratch(),
        compiler_params=pltpu.CompilerParams(
            dimension_semantics=("arbitrary",) * 3, vmem_limit_bytes=VMEM_LIMIT),
        name="mlaattn",
    )(mq, mk, mvt, mg)


def _out_kernel(x_ref, od_ref, om_ref, p_ref, wo_ref, wp_ref, wg_ref, fg_ref, o_ref):
    f32 = jnp.float32
    bf16 = jnp.bfloat16
    for r in (slice(0, HALF), slice(HALF, BLK)):
        h = (x_ref[r, :]
             + jnp.dot(od_ref[0, r, :], wo_ref[0:DIFF_WIDTH, :], preferred_element_type=f32)
             + jnp.dot(om_ref[0, r, :], wo_ref[DIFF_WIDTH:, :], preferred_element_type=f32))
        gate = jax.nn.sigmoid(jnp.dot(h.astype(bf16), wg_ref[...], preferred_element_type=f32))
        emb = jnp.dot(p_ref[r, :].astype(bf16), wp_ref[...], preferred_element_type=f32)
        h = h + emb * gate
        o_ref[r, :] = _rms(h, fg_ref[...])


def _out_call(x2, od, om, p2, wo, wp, wg, fg, nq):
    t = x2.shape[0]
    row = lambda w: pl.BlockSpec((BLK, w), lambda i: (i, 0))
    full = lambda a: pl.BlockSpec(a.shape, lambda i: (0, 0))
    fold = lambda w: pl.BlockSpec((1, BLK, w), lambda i: (i // nq * nq + _folded(i % nq, nq), 0, 0))
    return pl.pallas_call(
        _out_kernel,
        grid=(t // BLK,),
        in_specs=[row(D_MODEL), fold(DIFF_WIDTH), fold(MLA_WIDTH), row(PLE_DIM),
                  full(wo), full(wp), full(wg), full(fg)],
        out_specs=row(D_MODEL),
        out_shape=jax.ShapeDtypeStruct((t, D_MODEL), jnp.float32),
        compiler_params=pltpu.CompilerParams(
            dimension_semantics=("arbitrary",), vmem_limit_bytes=VMEM_LIMIT),
        name="outproj",
    )(x2, od, om, p2, wo, wp, wg, fg)


def _rope_tables():
    f32 = jnp.float32
    lane = jnp.arange(LANES)
    inv_d = ROPE_THETA ** (-jnp.arange(0, DIFF_HD, 2, dtype=f32) / DIFF_HD)
    half_d = DIFF_HD // 2
    inv_d_l = inv_d[lane % half_d]
    sp_d = jnp.where(lane % DIFF_HD >= half_d, 1.0, 0.0)
    sm_d = jnp.where(lane % DIFF_HD < half_d, -1.0, 0.0)
    inv_m = ROPE_THETA ** (-jnp.arange(0, MLA_ROPE, 2, dtype=f32) / MLA_ROPE)
    half_m = MLA_ROPE // 2
    in_rope = (lane >= MLA_NOPE) & (lane < MLA_NOPE + MLA_ROPE)
    inv_m_l = jnp.where(in_rope, inv_m[(lane - MLA_NOPE) % half_m], 0.0)
    sp_m = jnp.where(in_rope & (lane >= MLA_NOPE + half_m), 1.0, 0.0)
    sm_m = jnp.where(in_rope & (lane < MLA_NOPE + half_m), -1.0, 0.0)
    zero = jnp.zeros((LANES,), f32)
    return jnp.stack([inv_d_l, sp_d + sm_d, sp_d, inv_m_l, sp_m + sm_m, sp_m, zero, zero]).astype(f32)


def kernel(x, p, positions, norm_g, w_in, diff_lambda, diff_subln_g, mla_q_norm_g, w_uq,
           mla_kv_norm_g, w_ukv, w_out, w_ple, w_ple_gate, final_norm_g):
    b, s, d = x.shape
    t = b * s
    nq = s // BLK
    assert s % (2 * PAIRS * BLK) == 0
    f32 = jnp.float32
    bf16 = jnp.bfloat16
    i = 0

    offs = [0, 512, 1024, 1536, 2048, 2432, 2560, 2592, 3104]
    seg = [w_in[i][:, offs[j]:offs[j + 1]] for j in range(8)]
    wdq, wdk, wdv, wdg, wcq, wckv, wkr, wmg = seg
    zc = lambda n: jnp.zeros((d, n), f32)
    w_in_p = jnp.concatenate(
        [wdq, wdk, wdg, wmg, wcq, wckv, zc(MLA_NOPE), wkr, zc(LANES - MLA_NOPE - MLA_ROPE)],
        axis=1).astype(bf16)
    wuq_p = jnp.pad(w_uq[i].reshape(MLA_Q_LORA, MLA_HEADS, MLA_NOPE + MLA_ROPE),
                    ((0, 0), (0, 0), (0, LANES - MLA_NOPE - MLA_ROPE))
                    ).reshape(MLA_Q_LORA, MLA_HEADS * LANES).astype(bf16)
    wkv3 = w_ukv[i].reshape(MLA_KV_LORA, MLA_HEADS, MLA_NOPE + MLA_V)
    wk_p = jnp.pad(wkv3[:, :, :MLA_NOPE], ((0, 0), (0, 0), (0, LANES - MLA_NOPE))
                   ).reshape(MLA_KV_LORA, MLA_HEADS * LANES).astype(bf16)
    wvt_p = wkv3[:, :, MLA_NOPE:].reshape(MLA_KV_LORA, MLA_WIDTH).T.astype(bf16)
    wdvt = wdv.T.astype(bf16)

    x2 = x.reshape(t, d)
    posb = positions.astype(f32).reshape(t // BLK, 1, BLK)
    dq, dk, dvt, dg, mq, mk, mvt, mg = _proj_call(
        x2, posb, _rope_tables(), norm_g[i].reshape(1, d), w_in_p,
        mla_q_norm_g[i].reshape(1, MLA_Q_LORA), wuq_p,
        mla_kv_norm_g[i].reshape(1, MLA_KV_LORA), wk_p, wvt_p, wdvt, nq)

    r3 = lambda a: a.reshape(b, s, a.shape[-1])
    r4 = lambda a: a.reshape(b, nq, a.shape[-2], BLK)
    r5 = lambda a: a.reshape(b, 2, nq // 2, BLK, a.shape[-1])
    od = _diff_call(diff_lambda[i].astype(f32), r5(dq), r3(dk), r4(dvt), r5(dg),
                    diff_subln_g[i].reshape(1, 2 * DIFF_HD))
    om = _mla_call(r5(mq), r3(mk), r4(mvt), r5(mg))

    out = _out_call(x2, od.reshape(t // BLK, BLK, DIFF_WIDTH), om.reshape(t // BLK, BLK, MLA_WIDTH),
                    p[i].reshape(t, PLE_DIM), w_out[i].astype(bf16), w_ple[i].astype(bf16),
                    w_ple_gate[i].astype(bf16), final_norm_g.reshape(1, d), nq)
    return out.reshape(b, s, d)
```

```python
import functools
import math

import jax
import jax.numpy as jnp
from jax import lax
from jax.experimental import pallas as pl
from jax.experimental.pallas import tpu as pltpu

D_MODEL = 1024
PLE_DIM = 256
ROPE_THETA = 10000.0
RMS_EPS = 1e-6

DIFF_WIDTH = 512
DIFF_HEADS = 4
DIFF_HD = 64
MLA_WIDTH = 512
MLA_HEADS = 8
MLA_NOPE = 64
MLA_ROPE = 32
MLA_V = 64
MLA_Q_LORA = 384
MLA_KV_LORA = 128

LANES = 128
LOG2E = math.log2(math.e)
VMEM_LIMIT = 48 * 1024 * 1024

O_DQ, O_DK, O_DG, O_MG = 0, 512, 1024, 1536
O_CQ = 2048
O_CKV = O_CQ + MLA_Q_LORA
O_KR = O_CKV + MLA_KV_LORA
D_IN_P = O_KR + LANES

_NT = (((1,), (1,)), ((), ()))
BLK = 512
HALF = BLK // 2
PAIRS = 2
AHEAD = 3


def _rms(x, g):
    return x * lax.rsqrt(jnp.mean(x * x, axis=-1, keepdims=True) + RMS_EPS) * g


def _silu(x):
    return x / (1.0 + jnp.exp(-x))


def _rope128(x, c, s, below, half):
    partner = jnp.where(below, pltpu.roll(x, half, axis=1), pltpu.roll(x, LANES - half, axis=1))
    return x * c + partner * s


def _folded(c, n):
    return jnp.where(c < n // 2, c, n + n // 2 - 1 - c)


def _proj_kernel(x_ref, pos_ref, posprev_ref, tab_ref, ng_ref, w_ref, qg_ref, wuq_ref, kvg_ref,
                 wk_ref, wvt_ref, wdv_ref,
                 dq_ref, dk_ref, dvt_ref, dg_ref, mq_ref, mk_ref, mvt_ref, mg_ref,
                 rowtab_ref, rope_ref, nb_ref, flag_ref, wdvt_ref):
    f32 = jnp.float32
    bf16 = jnp.bfloat16
    i = pl.program_id(0)
    row = lax.broadcasted_iota(jnp.int32, (BLK, LANES), 0).astype(f32)

    def store_tables(slot, t, c, s):
        rope_ref[slot, 2 * t] = c
        rope_ref[slot, 2 * t + 1] = s * tab_ref[3 * t + 1:3 * t + 2, :]

    def prep(slot):
        nb_ref[slot] = _rms(x_ref[...], ng_ref[...]).astype(bf16)
        pos = pos_ref[0]
        base = pos[:, 0:1]
        idx = lax.broadcasted_iota(jnp.int32, (1, BLK), 1).astype(f32)
        flag_ref[slot] = jnp.sum(jnp.where(pos == base + idx, 0.0, 1.0))
        for t in range(2):
            ang = base * tab_ref[3 * t:3 * t + 1, :]
            cb, sb = jnp.cos(ang), jnp.sin(ang)
            cr, sr = rowtab_ref[2 * t], rowtab_ref[2 * t + 1]
            store_tables(slot, t, cb * cr - sb * sr, sb * cr + cb * sr)

    def direct_tables(slot):
        pos = jnp.broadcast_to(posprev_ref[0], (LANES, BLK)).T
        for t in range(2):
            ang = pos * tab_ref[3 * t:3 * t + 1, :]
            store_tables(slot, t, jnp.cos(ang), jnp.sin(ang))

    def main(slot):
        nb = nb_ref[slot]
        cd, sd, below_d = rope_ref[slot, 0], rope_ref[slot, 1], tab_ref[2:3, :] > 0.0
        cm, sm, below_m = rope_ref[slot, 2], rope_ref[slot, 3], tab_ref[5:6, :] > 0.0

        def proj(off, width):
            return jnp.dot(nb, w_ref[:, off:off + width], preferred_element_type=f32)

        lat = proj(O_CQ, D_IN_P - O_CQ)
        q_scale = DIFF_HD ** -0.5 * LOG2E
        dq = proj(O_DQ, DIFF_WIDTH)
        dk = proj(O_DK, DIFF_WIDTH)
        cqn = _rms(lat[:, :MLA_Q_LORA], qg_ref[...]).astype(bf16)
        ckvn = _rms(lat[:, O_CKV - O_CQ:O_KR - O_CQ], kvg_ref[...]).astype(bf16)
        for c in range(DIFF_WIDTH // LANES):
            sl = slice(c * LANES, (c + 1) * LANES)
            dq_ref[0, :, sl] = (_rope128(dq[:, sl], cd, sd, below_d, DIFF_HD // 2)
                                * q_scale).astype(bf16)
            dk_ref[:, sl] = _rope128(dk[:, sl], cd, sd, below_d, DIFF_HD // 2).astype(bf16)

        mq = jnp.dot(cqn, wuq_ref[...], preferred_element_type=f32)
        m_scale = (MLA_NOPE + MLA_ROPE) ** -0.5 * LOG2E
        kn = jnp.dot(ckvn, wk_ref[...], preferred_element_type=f32)
        kr = _rope128(lat[:, O_KR - O_CQ:], cm, sm, below_m, MLA_ROPE // 2)
        for h in range(MLA_HEADS):
            sl = slice(h * LANES, (h + 1) * LANES)
            mq_ref[0, :, sl] = (_rope128(mq[:, sl], cm, sm, below_m, MLA_ROPE // 2)
                                * m_scale).astype(bf16)
            mk_ref[:, sl] = (kn[:, sl] + kr).astype(bf16)

        dg_ref[0] = _silu(proj(O_DG, DIFF_WIDTH)).astype(bf16)
        mg_ref[0] = _silu(proj(O_MG, MLA_WIDTH)).astype(bf16)
        mvt_ref[0] = lax.dot_general(wvt_ref[...], ckvn, _NT,
                                     preferred_element_type=f32).astype(bf16)
        dvt_ref[0] = lax.dot_general(wdvt_ref[...], nb, _NT,
                                     preferred_element_type=f32).astype(bf16)

    def step(cur):
        prv = 1 - cur

        @pl.when(flag_ref[prv] != 0.0)
        def _():
            direct_tables(prv)

        main(prv)
        prep(cur)

    @pl.when(i == 0)
    def _():
        for t in range(2):
            ang = row * tab_ref[3 * t:3 * t + 1, :]
            rowtab_ref[2 * t] = jnp.cos(ang)
            rowtab_ref[2 * t + 1] = jnp.sin(ang)
        wdvt_ref[...] = wdv_ref[...].T.astype(bf16)
        prep(0)

    @pl.when(i % 2 == 1)
    def _():
        step(1)

    @pl.when((i > 0) & (i % 2 == 0))
    def _():
        step(0)


def _proj_call(x2, posb, tab, ng, w_in_p, qg, wuq_p, kvg, wk_p, wvt_p, wdv, nq):
    t = x2.shape[0]
    n = t // BLK
    bf16 = jnp.bfloat16
    nxt = lambda i: jnp.minimum(i, n - 1)
    cur = lambda i: jnp.maximum(i - 1, 0)
    row_in = lambda w, blk: pl.BlockSpec((BLK, w), lambda i: (blk(i), 0))
    pos_in = lambda blk: pl.BlockSpec((1, 1, BLK), lambda i: (blk(i), 0, 0))
    full = lambda a: pl.BlockSpec(a.shape, lambda i: (0, 0))
    row = lambda w: pl.BlockSpec((BLK, w), lambda i: (cur(i), 0))
    blk3 = lambda w: pl.BlockSpec((1, w, BLK), lambda i: (cur(i), 0, 0))
    fold = lambda w: pl.BlockSpec(
        (1, BLK, w), lambda i: (cur(i) // nq * nq + _folded(cur(i) % nq, nq), 0, 0))
    rows = lambda w: jax.ShapeDtypeStruct((t, w), bf16)
    blks = lambda w: jax.ShapeDtypeStruct((n, w, BLK), bf16)
    folds = lambda w: jax.ShapeDtypeStruct((n, BLK, w), bf16)
    return pl.pallas_call(
        _proj_kernel,
        grid=(n + 1,),
        in_specs=[row_in(D_MODEL, nxt), pos_in(nxt), pos_in(cur), full(tab),
                  full(ng), full(w_in_p), full(qg), full(wuq_p), full(kvg), full(wk_p),
                  full(wvt_p), full(wdv)],
        out_specs=[fold(DIFF_WIDTH), row(DIFF_WIDTH), blk3(DIFF_WIDTH), fold(DIFF_WIDTH),
                   fold(MLA_HEADS * LANES), row(MLA_HEADS * LANES), blk3(MLA_WIDTH),
                   fold(MLA_WIDTH)],
        out_shape=[folds(DIFF_WIDTH), rows(DIFF_WIDTH), blks(DIFF_WIDTH), folds(DIFF_WIDTH),
                   folds(MLA_HEADS * LANES), rows(MLA_HEADS * LANES), blks(MLA_WIDTH),
                   folds(MLA_WIDTH)],
        scratch_shapes=[pltpu.VMEM((4, BLK, LANES), jnp.float32),
                        pltpu.VMEM((2, 4, BLK, LANES), jnp.float32),
                        pltpu.VMEM((2, BLK, D_MODEL), bf16),
                        pltpu.SMEM((2,), jnp.float32),
                        pltpu.VMEM((DIFF_WIDTH, D_MODEL), bf16)],
        compiler_params=pltpu.CompilerParams(
            dimension_semantics=("arbitrary",), vmem_limit_bytes=VMEM_LIMIT),
        name="proj",
    )(x2, posb, posb, tab, ng, w_in_p, qg, wuq_p, kvg, wk_p, wvt_p, wdv)


SUM_LO, SUM_HI = 2.0 ** -64, 2.0 ** 64
OUT_LIMIT = 2.0 ** 100


def _attn_fast(nq, get_q, get_k, vt_ref, acc_ref, l_ref, epilogue):
    f32 = jnp.float32
    first = pl.program_id(2) * PAIRS

    _zero(acc_ref, l_ref)

    def diagonal(side, j):
        out = []
        for i in range(2):
            out += [(side, j, i, 0, HALF, 0, HALF, True),
                    (side, j, i, 0, HALF, HALF, HALF, False),
                    (side, j, i, HALF, HALF, HALF, HALF, True)]
        return out

    def full(side, j):
        return [(side, j, i, 0, BLK, q0, HALF, False) for i in range(2) for q0 in (0, HALF)]

    units, done = [], {}
    for p in range(PAIRS):
        c = first + p
        for t in range(nq // 2 - 1):
            on_side0 = t < c
            units += full((p, jnp.where(on_side0, 0, 1)), jnp.where(on_side0, t, t - c))
        units += diagonal((p, 0), c)
        done[len(units)] = (p, 0)
        units += diagonal((p, 1), nq - 1 - c)
        for t in range(nq // 2 - 1, nq - 1):
            units += full((p, 1), t - c)
        done[len(units)] = (p, 1)

    def qk(u):
        side, j, i, k0, nk, q0, nq_ = units[u][:7]
        return lax.dot_general(get_k(i, j, k0, nk), get_q(side, i, q0, nq_), _NT,
                               preferred_element_type=f32)

    bad = jnp.zeros((1, BLK), f32)
    scores = {u: qk(u) for u in range(AHEAD)}
    for u, (side, j, i, k0, nk, q0, nq_, masked) in enumerate(units):
        s = scores.pop(u)
        if masked:
            krow = lax.broadcasted_iota(jnp.int32, s.shape, 0) + k0
            qcol = lax.broadcasted_iota(jnp.int32, s.shape, 1) + q0
            s = jnp.where(krow <= qcol, s, -jnp.inf)
        p = jnp.exp2(s)
        col = jnp.sum(p, axis=0, keepdims=True)
        pb = p.astype(jnp.bfloat16)
        if u + AHEAD < len(units):
            scores[u + AHEAD] = qk(u + AHEAD)
        pv = jnp.dot(vt_ref[0, j, :, k0:k0 + nk], pb, preferred_element_type=f32)
        where = side + (i, slice(None), slice(q0, q0 + nq_))
        l_ref[where] += col
        acc_ref[where] += pv
        if u + 1 in done:
            bad = bad + epilogue(done[u + 1])
    return bad


def _sides():
    return [(p, s) for p in range(PAIRS) for s in range(2)]


def _attn_rescaling(nq, get_q, get_k, vt_ref, acc_ref, l_ref):
    f32 = jnp.float32
    first = pl.program_id(2) * PAIRS
    acc_ref[...] = jnp.zeros(acc_ref.shape, f32)

    for side in _sides():
        n_full = first + side[0] if side[1] == 0 else nq - 1 - first - side[0]

        def step(j, carry, masked, side=side):
            vt = vt_ref[0, j]
            out = []
            for i in range(2):
                m_prev, l_prev = carry[2 * i], carry[2 * i + 1]
                s = lax.dot_general(get_k(i, j, 0, BLK), get_q(side, i, 0, BLK), _NT,
                                    preferred_element_type=f32)
                if masked:
                    krow = lax.broadcasted_iota(jnp.int32, s.shape, 0)
                    qcol = lax.broadcasted_iota(jnp.int32, s.shape, 1)
                    s = jnp.where(krow <= qcol, s, -jnp.inf)
                m_new = jnp.maximum(m_prev, jnp.max(s, axis=0, keepdims=True))
                alpha = jnp.exp2(m_prev - m_new)
                p = jnp.exp2(s - m_new)
                l_new = alpha * l_prev + jnp.sum(p, axis=0, keepdims=True)
                acc_ref[side + (i,)] = alpha * acc_ref[side + (i,)] + jnp.dot(
                    vt, p.astype(jnp.bfloat16), preferred_element_type=f32)
                out += [m_new, l_new]
            return tuple(out)

        neg = jnp.full((1, BLK), -jnp.inf, f32)
        zero = jnp.zeros((1, BLK), f32)
        carry = lax.fori_loop(0, n_full, lambda j, cr: step(j, cr, False),
                              (neg, zero, neg, zero))
        carry = step(n_full, carry, True)
        l_ref[side + (0,)] = carry[1]
        l_ref[side + (1,)] = carry[3]


def _sums_out_of_range(l0, l1):
    ok = ((l0 > SUM_LO) & (l0 < SUM_HI)) & ((l1 > SUM_LO) & (l1 < SUM_HI))
    return jnp.where(ok, 0.0, 1.0)


def _guarded(nq, get_q, get_k, vt_ref, acc_ref, l_ref, epilogue):
    n_bad = jnp.sum(_attn_fast(nq, get_q, get_k, vt_ref, acc_ref, l_ref, epilogue))

    @pl.when(n_bad > 0.0)
    def _():
        _attn_rescaling(nq, get_q, get_k, vt_ref, acc_ref, l_ref)
        for side in _sides():
            epilogue(side)


def _zero(*refs):
    for ref in refs:
        ref[...] = jnp.zeros(ref.shape, ref.dtype)


def _key_rows(j, k0, n):
    return pl.ds(pl.multiple_of(j * BLK + k0, HALF), n)


def _diff_kernel(nq, lam_ref, q_ref, k_ref, vt_ref, g_ref, sg_ref, o_ref, qz_ref, acc_ref, l_ref):
    f32 = jnp.float32
    for p, s in _sides():
        q = q_ref[0, s, p]
        lane = lax.broadcasted_iota(jnp.int32, q.shape, 1)
        zero = jnp.zeros_like(q)
        qz_ref[p, s, 0] = jnp.where(lane < DIFF_HD, q, zero)
        qz_ref[p, s, 1] = jnp.where(lane >= DIFF_HD, q, zero)

    lam_rows = lam_ref[...]
    lam_init = 0.8 - 0.6 * math.exp(-0.3 * 0)
    lam = (jnp.exp(jnp.sum(lam_rows[0:1] * lam_rows[1:2], axis=1, keepdims=True))
           - jnp.exp(jnp.sum(lam_rows[2:3] * lam_rows[3:4], axis=1, keepdims=True))
           + lam_init)

    def epilogue(side):
        p, s = side
        l0, l1 = l_ref[p, s, 0], l_ref[p, s, 1]
        ot = acc_ref[p, s, 0] * (1.0 / l0) - acc_ref[p, s, 1] * (lam / l1)
        ms = jnp.mean(ot * ot, axis=0, keepdims=True)
        ot = ot * lax.rsqrt(ms + RMS_EPS)
        o = ot.T * (sg_ref[...] * (1.0 - lam_init)) * g_ref[0, s, p].astype(f32)
        o_ref[0, s, p] = o.astype(o_ref.dtype)
        return _sums_out_of_range(l0, l1) + jnp.where(ms < OUT_LIMIT, 0.0, 1.0)

    _guarded(nq,
             lambda side, i, q0, n: qz_ref[side[0], side[1], i, q0:q0 + n, :],
             lambda i, j, k0, n: k_ref[0, _key_rows(j, k0, n), :],
             vt_ref, acc_ref, l_ref, epilogue)


def _mla_kernel(nq, q_ref, k_ref, vt_ref, g_ref, o_ref, acc_ref, l_ref):
    f32 = jnp.float32

    def epilogue(side):
        p, s = side
        l0, l1 = l_ref[p, s, 0], l_ref[p, s, 1]
        feat = lax.broadcasted_iota(jnp.int32, (LANES, BLK), 0)
        ot = jnp.where(feat < MLA_V, acc_ref[p, s, 0] * (1.0 / l0), acc_ref[p, s, 1] * (1.0 / l1))
        peak = jnp.max(jnp.abs(ot), axis=0, keepdims=True)
        o_ref[0, s, p] = (ot.T * g_ref[0, s, p].astype(f32)).astype(o_ref.dtype)
        return _sums_out_of_range(l0, l1) + jnp.where(peak < OUT_LIMIT, 0.0, 1.0)

    _guarded(nq,
             lambda side, i, q0, n: q_ref[0, side[1], side[0], q0:q0 + n,
                                          i * LANES:(i + 1) * LANES],
             lambda i, j, k0, n: k_ref[0, _key_rows(j, k0, n), i * LANES:(i + 1) * LANES],
             vt_ref, acc_ref, l_ref, epilogue)


def _attn_scratch():
    return [pltpu.VMEM((PAIRS, 2, 2, LANES, BLK), jnp.float32),
            pltpu.VMEM((PAIRS, 2, 2, 1, BLK), jnp.float32)]


def _pair_spec(width):
    return pl.BlockSpec((1, 2, PAIRS, BLK, width), lambda bi, h, c: (bi, 0, c, 0, h))


def _diff_call(lam_rows, dq, dk, dvt, dg, subg):
    b, _, npair, _, _ = dq.shape
    nq = 2 * npair
    s = nq * BLK
    kspec = pl.BlockSpec((1, s, LANES), lambda bi, h, c: (bi, 0, h))
    vtspec = pl.BlockSpec((1, nq, LANES, BLK), lambda bi, h, c: (bi, 0, h, 0))
    return pl.pallas_call(
        functools.partial(_diff_kernel, nq),
        grid=(b, DIFF_HEADS, npair // PAIRS),
        in_specs=[pl.BlockSpec(lam_rows.shape, lambda bi, h, c: (0, 0)),
                  _pair_spec(LANES), kspec, vtspec, _pair_spec(LANES),
                  pl.BlockSpec(subg.shape, lambda bi, h, c: (0, 0))],
        out_specs=_pair_spec(LANES),
        out_shape=jax.ShapeDtypeStruct(dq.shape, jnp.bfloat16),
        scratch_shapes=[pltpu.VMEM((PAIRS, 2, 2, BLK, LANES), jnp.bfloat16)] + _attn_scratch(),
        compiler_params=pltpu.CompilerParams(
            dimension_semantics=("arbitrary",) * 3, vmem_limit_bytes=VMEM_LIMIT),
        name="diffattn",
    )(lam_rows, dq, dk, dvt, dg, subg)


def _mla_call(mq, mk, mvt, mg):
    b, _, npair, _, _ = mq.shape
    nq = 2 * npair
    s = nq * BLK
    return pl.pallas_call(
        functools.partial(_mla_kernel, nq),
        grid=(b, MLA_HEADS // 2, npair // PAIRS),
        in_specs=[_pair_spec(2 * LANES),
                  pl.BlockSpec((1, s, 2 * LANES), lambda bi, h, c: (bi, 0, h)),
                  pl.BlockSpec((1, nq, LANES, BLK), lambda bi, h, c: (bi, 0, h, 0)),
                  _pair_spec(LANES)],
        out_specs=_pair_spec(LANES),
        out_shape=jax.ShapeDtypeStruct(mg.shape, jnp.bfloat16),
        scratch_shapes=_attn_scratch(),
        compiler_params=pltpu.CompilerParams(
            dimension_semantics=("arbitrary",) * 3, vmem_limit_bytes=VMEM_LIMIT),
        name="mlaattn",
    )(mq, mk, mvt, mg)


def _out_kernel(x_ref, od_ref, om_ref, p_ref, wo32_ref, wp32_ref, wg32_ref, fg_ref, o_ref,
                wo_ref, wp_ref, wg_ref):
    f32 = jnp.float32
    bf16 = jnp.bfloat16

    @pl.when(pl.program_id(0) == 0)
    def _():
        wo_ref[...] = wo32_ref[...].astype(bf16)
        wp_ref[...] = wp32_ref[...].astype(bf16)
        wg_ref[...] = wg32_ref[...].astype(bf16)

    for r in (slice(0, HALF), slice(HALF, BLK)):
        h = (x_ref[r, :]
             + jnp.dot(od_ref[0, r, :], wo_ref[0:DIFF_WIDTH, :], preferred_element_type=f32)
             + jnp.dot(om_ref[0, r, :], wo_ref[DIFF_WIDTH:, :], preferred_element_type=f32))
        gate = jax.nn.sigmoid(jnp.dot(h.astype(bf16), wg_ref[...], preferred_element_type=f32))
        emb = jnp.dot(p_ref[r, :].astype(bf16), wp_ref[...], preferred_element_type=f32)
        h = h + emb * gate
        o_ref[r, :] = _rms(h, fg_ref[...])


def _out_call(x2, od, om, p2, wo, wp, wg, fg, nq):
    t = x2.shape[0]
    row = lambda w: pl.BlockSpec((BLK, w), lambda i: (i, 0))
    full = lambda a: pl.BlockSpec(a.shape, lambda i: (0, 0))
    fold = lambda w: pl.BlockSpec((1, BLK, w), lambda i: (i // nq * nq + _folded(i % nq, nq), 0, 0))
    return pl.pallas_call(
        _out_kernel,
        grid=(t // BLK,),
        in_specs=[row(D_MODEL), fold(DIFF_WIDTH), fold(MLA_WIDTH), row(PLE_DIM),
                  full(wo), full(wp), full(wg), full(fg)],
        out_specs=row(D_MODEL),
        out_shape=jax.ShapeDtypeStruct((t, D_MODEL), jnp.float32),
        scratch_shapes=[pltpu.VMEM(w.shape, jnp.bfloat16) for w in (wo, wp, wg)],
        compiler_params=pltpu.CompilerParams(
            dimension_semantics=("arbitrary",), vmem_limit_bytes=VMEM_LIMIT),
        name="outproj",
    )(x2, od, om, p2, wo, wp, wg, fg)


def _rope_tables():
    f32 = jnp.float32
    lane = jnp.arange(LANES)
    inv_d = ROPE_THETA ** (-jnp.arange(0, DIFF_HD, 2, dtype=f32) / DIFF_HD)
    half_d = DIFF_HD // 2
    inv_d_l = inv_d[lane % half_d]
    sp_d = jnp.where(lane % DIFF_HD >= half_d, 1.0, 0.0)
    sm_d = jnp.where(lane % DIFF_HD < half_d, -1.0, 0.0)
    inv_m = ROPE_THETA ** (-jnp.arange(0, MLA_ROPE, 2, dtype=f32) / MLA_ROPE)
    half_m = MLA_ROPE // 2
    in_rope = (lane >= MLA_NOPE) & (lane < MLA_NOPE + MLA_ROPE)
    inv_m_l = jnp.where(in_rope, inv_m[(lane - MLA_NOPE) % half_m], 0.0)
    sp_m = jnp.where(in_rope & (lane >= MLA_NOPE + half_m), 1.0, 0.0)
    sm_m = jnp.where(in_rope & (lane < MLA_NOPE + half_m), -1.0, 0.0)
    zero = jnp.zeros((LANES,), f32)
    return jnp.stack([inv_d_l, sp_d + sm_d, sp_d, inv_m_l, sp_m + sm_m, sp_m, zero, zero]).astype(f32)


def kernel(x, p, positions, norm_g, w_in, diff_lambda, diff_subln_g, mla_q_norm_g, w_uq,
           mla_kv_norm_g, w_ukv, w_out, w_ple, w_ple_gate, final_norm_g):
    b, s, d = x.shape
    t = b * s
    nq = s // BLK
    assert s % (2 * PAIRS * BLK) == 0
    f32 = jnp.float32
    bf16 = jnp.bfloat16
    i = 0

    offs = [0, 512, 1024, 1536, 2048, 2432, 2560, 2592, 3104]
    seg = [w_in[i][:, offs[j]:offs[j + 1]] for j in range(8)]
    wdq, wdk, wdv, wdg, wcq, wckv, wkr, wmg = seg
    zc = lambda n: jnp.zeros((d, n), f32)
    w_in_p = jnp.concatenate(
        [wdq, wdk, wdg, wmg, wcq, wckv, zc(MLA_NOPE), wkr, zc(LANES - MLA_NOPE - MLA_ROPE)],
        axis=1).astype(bf16)
    wuq_p = jnp.pad(w_uq[i].reshape(MLA_Q_LORA, MLA_HEADS, MLA_NOPE + MLA_ROPE),
                    ((0, 0), (0, 0), (0, LANES - MLA_NOPE - MLA_ROPE))
                    ).reshape(MLA_Q_LORA, MLA_HEADS * LANES).astype(bf16)
    wkv3 = w_ukv[i].reshape(MLA_KV_LORA, MLA_HEADS, MLA_NOPE + MLA_V)
    wk_p = jnp.pad(wkv3[:, :, :MLA_NOPE], ((0, 0), (0, 0), (0, LANES - MLA_NOPE))
                   ).reshape(MLA_KV_LORA, MLA_HEADS * LANES).astype(bf16)
    wvt_p = wkv3[:, :, MLA_NOPE:].reshape(MLA_KV_LORA, MLA_WIDTH).T.astype(bf16)

    x2 = x.reshape(t, d)
    posb = positions.astype(f32).reshape(t // BLK, 1, BLK)
    dq, dk, dvt, dg, mq, mk, mvt, mg = _proj_call(
        x2, posb, _rope_tables(), norm_g[i].reshape(1, d), w_in_p,
        mla_q_norm_g[i].reshape(1, MLA_Q_LORA), wuq_p,
        mla_kv_norm_g[i].reshape(1, MLA_KV_LORA), wk_p, wvt_p, wdv, nq)

    r3 = lambda a: a.reshape(b, s, a.shape[-1])
    r4 = lambda a: a.reshape(b, nq, a.shape[-2], BLK)
    r5 = lambda a: a.reshape(b, 2, nq // 2, BLK, a.shape[-1])
    od = _diff_call(diff_lambda[i].astype(f32), r5(dq), r3(dk), r4(dvt), r5(dg),
                    diff_subln_g[i].reshape(1, 2 * DIFF_HD))
    om = _mla_call(r5(mq), r3(mk), r4(mvt), r5(mg))

    out = _out_call(x2, od.reshape(t // BLK, BLK, DIFF_WIDTH), om.reshape(t // BLK, BLK, MLA_WIDTH),
                    p[i].reshape(t, PLE_DIM), w_out[i], w_ple[i], w_ple_gate[i],
                    final_norm_g.reshape(1, d), nq)
    return out.reshape(b, s, d)
```

```python
import functools
import math

import jax
import jax.numpy as jnp
from jax import lax
from jax.experimental import pallas as pl
from jax.experimental.pallas import tpu as pltpu

D_MODEL = 1024
PLE_DIM = 256
ROPE_THETA = 10000.0
RMS_EPS = 1e-6

DIFF_WIDTH = 512
DIFF_HEADS = 4
DIFF_HD = 64
MLA_WIDTH = 512
MLA_HEADS = 8
MLA_NOPE = 64
MLA_ROPE = 32
MLA_V = 64
MLA_Q_LORA = 384
MLA_KV_LORA = 128

LANES = 128
LOG2E = math.log2(math.e)
VMEM_LIMIT = 48 * 1024 * 1024

O_DQ, O_DK, O_DG, O_MG = 0, 512, 1024, 1536
O_CQ = 2048
O_CKV = O_CQ + MLA_Q_LORA
O_KR = O_CKV + MLA_KV_LORA
D_IN_P = O_KR + LANES

_NT = (((1,), (1,)), ((), ()))
BLK = 512
HALF = BLK // 2
PAIRS = 4
AHEAD = 3


def _rms(x, g):
    return x * lax.rsqrt(jnp.mean(x * x, axis=-1, keepdims=True) + RMS_EPS) * g


def _silu(x):
    return x / (1.0 + jnp.exp(-x))


def _rope128(x, c, s, below, half):
    partner = jnp.where(below, pltpu.roll(x, half, axis=1), pltpu.roll(x, LANES - half, axis=1))
    return x * c + partner * s


def _folded(c, n):
    return jnp.where(c < n // 2, c, n + n // 2 - 1 - c)


def _proj_kernel(x_ref, pos_ref, posprev_ref, tab_ref, ng_ref, w_ref, qg_ref, wuq_ref, kvg_ref,
                 wk_ref, wvt_ref, wdvt_ref,
                 dq_ref, dk_ref, dvt_ref, dg_ref, mq_ref, mk_ref, mvt_ref, mg_ref,
                 rowtab_ref, rope_ref, nb_ref, flag_ref):
    f32 = jnp.float32
    bf16 = jnp.bfloat16
    i = pl.program_id(0)
    row = lax.broadcasted_iota(jnp.int32, (BLK, LANES), 0).astype(f32)

    def store_tables(slot, t, c, s):
        rope_ref[slot, 2 * t] = c
        rope_ref[slot, 2 * t + 1] = s * tab_ref[3 * t + 1:3 * t + 2, :]

    def prep(slot):
        nb_ref[slot] = _rms(x_ref[...], ng_ref[...]).astype(bf16)
        pos = pos_ref[0]
        base = pos[:, 0:1]
        idx = lax.broadcasted_iota(jnp.int32, (1, BLK), 1).astype(f32)
        flag_ref[slot] = jnp.sum(jnp.where(pos == base + idx, 0.0, 1.0))
        for t in range(2):
            ang = base * tab_ref[3 * t:3 * t + 1, :]
            cb, sb = jnp.cos(ang), jnp.sin(ang)
            cr, sr = rowtab_ref[2 * t], rowtab_ref[2 * t + 1]
            store_tables(slot, t, cb * cr - sb * sr, sb * cr + cb * sr)

    def direct_tables(slot):
        pos = jnp.broadcast_to(posprev_ref[0], (LANES, BLK)).T
        for t in range(2):
            ang = pos * tab_ref[3 * t:3 * t + 1, :]
            store_tables(slot, t, jnp.cos(ang), jnp.sin(ang))

    def main(slot):
        nb = nb_ref[slot]
        cd, sd, below_d = rope_ref[slot, 0], rope_ref[slot, 1], tab_ref[2:3, :] > 0.0
        cm, sm, below_m = rope_ref[slot, 2], rope_ref[slot, 3], tab_ref[5:6, :] > 0.0

        def proj(off, width):
            return jnp.dot(nb, w_ref[:, off:off + width], preferred_element_type=f32)

        lat = proj(O_CQ, D_IN_P - O_CQ)
        q_scale = DIFF_HD ** -0.5 * LOG2E
        dq = proj(O_DQ, DIFF_WIDTH)
        dk = proj(O_DK, DIFF_WIDTH)
        cqn = _rms(lat[:, :MLA_Q_LORA], qg_ref[...]).astype(bf16)
        ckvn = _rms(lat[:, O_CKV - O_CQ:O_KR - O_CQ], kvg_ref[...]).astype(bf16)
        for c in range(DIFF_WIDTH // LANES):
            sl = slice(c * LANES, (c + 1) * LANES)
            dq_ref[0, :, sl] = (_rope128(dq[:, sl], cd, sd, below_d, DIFF_HD // 2)
                                * q_scale).astype(bf16)
            dk_ref[:, sl] = _rope128(dk[:, sl], cd, sd, below_d, DIFF_HD // 2).astype(bf16)

        mq = jnp.dot(cqn, wuq_ref[...], preferred_element_type=f32)
        m_scale = (MLA_NOPE + MLA_ROPE) ** -0.5 * LOG2E
        kn = jnp.dot(ckvn, wk_ref[...], preferred_element_type=f32)
        kr = _rope128(lat[:, O_KR - O_CQ:], cm, sm, below_m, MLA_ROPE // 2)
        for h in range(MLA_HEADS):
            sl = slice(h * LANES, (h + 1) * LANES)
            mq_ref[0, :, sl] = (_rope128(mq[:, sl], cm, sm, below_m, MLA_ROPE // 2)
                                * m_scale).astype(bf16)
            mk_ref[:, sl] = (kn[:, sl] + kr).astype(bf16)

        dg_ref[0] = _silu(proj(O_DG, DIFF_WIDTH)).astype(bf16)
        mg_ref[0] = _silu(proj(O_MG, MLA_WIDTH)).astype(bf16)
        mvt_ref[0] = lax.dot_general(wvt_ref[...], ckvn, _NT,
                                     preferred_element_type=f32).astype(bf16)
        dvt_ref[0] = lax.dot_general(wdvt_ref[...], nb, _NT,
                                     preferred_element_type=f32).astype(bf16)

    def step(cur):
        prv = 1 - cur

        @pl.when(flag_ref[prv] != 0.0)
        def _():
            direct_tables(prv)

        main(prv)
        prep(cur)

    @pl.when(i == 0)
    def _():
        for t in range(2):
            ang = row * tab_ref[3 * t:3 * t + 1, :]
            rowtab_ref[2 * t] = jnp.cos(ang)
            rowtab_ref[2 * t + 1] = jnp.sin(ang)
        prep(0)

    @pl.when(i % 2 == 1)
    def _():
        step(1)

    @pl.when((i > 0) & (i % 2 == 0))
    def _():
        step(0)


def _proj_call(x2, posb, tab, ng, w_in_p, qg, wuq_p, kvg, wk_p, wvt_p, wdvt, nq):
    t = x2.shape[0]
    n = t // BLK
    bf16 = jnp.bfloat16
    nxt = lambda i: jnp.minimum(i, n - 1)
    cur = lambda i: jnp.maximum(i - 1, 0)
    row_in = lambda w, blk: pl.BlockSpec((BLK, w), lambda i: (blk(i), 0))
    pos_in = lambda blk: pl.BlockSpec((1, 1, BLK), lambda i: (blk(i), 0, 0))
    full = lambda a: pl.BlockSpec(a.shape, lambda i: (0, 0))
    row = lambda w: pl.BlockSpec((BLK, w), lambda i: (cur(i), 0))
    blk3 = lambda w: pl.BlockSpec((1, w, BLK), lambda i: (cur(i), 0, 0))
    fold = lambda w: pl.BlockSpec(
        (1, BLK, w), lambda i: (cur(i) // nq * nq + _folded(cur(i) % nq, nq), 0, 0))
    rows = lambda w: jax.ShapeDtypeStruct((t, w), bf16)
    blks = lambda w: jax.ShapeDtypeStruct((n, w, BLK), bf16)
    folds = lambda w: jax.ShapeDtypeStruct((n, BLK, w), bf16)
    return pl.pallas_call(
        _proj_kernel,
        grid=(n + 1,),
        in_specs=[row_in(D_MODEL, nxt), pos_in(nxt), pos_in(cur), full(tab),
                  full(ng), full(w_in_p), full(qg), full(wuq_p), full(kvg), full(wk_p),
                  full(wvt_p), full(wdvt)],
        out_specs=[fold(DIFF_WIDTH), row(DIFF_WIDTH), blk3(DIFF_WIDTH), fold(DIFF_WIDTH),
                   fold(MLA_HEADS * LANES), row(MLA_HEADS * LANES), blk3(MLA_WIDTH),
                   fold(MLA_WIDTH)],
        out_shape=[folds(DIFF_WIDTH), rows(DIFF_WIDTH), blks(DIFF_WIDTH), folds(DIFF_WIDTH),
                   folds(MLA_HEADS * LANES), rows(MLA_HEADS * LANES), blks(MLA_WIDTH),
                   folds(MLA_WIDTH)],
        scratch_shapes=[pltpu.VMEM((4, BLK, LANES), jnp.float32),
                        pltpu.VMEM((2, 4, BLK, LANES), jnp.float32),
                        pltpu.VMEM((2, BLK, D_MODEL), bf16),
                        pltpu.SMEM((2,), jnp.float32)],
        compiler_params=pltpu.CompilerParams(
            dimension_semantics=("arbitrary",), vmem_limit_bytes=VMEM_LIMIT),
        name="proj",
    )(x2, posb, posb, tab, ng, w_in_p, qg, wuq_p, kvg, wk_p, wvt_p, wdvt)


SUM_LO, SUM_HI = 2.0 ** -64, 2.0 ** 64
OUT_LIMIT = 2.0 ** 100


def _attn_fast(nq, get_q, get_k, vt_ref, acc_ref, l_ref, epilogue):
    f32 = jnp.float32
    first = pl.program_id(2) * PAIRS

    _zero(acc_ref, l_ref)

    def diagonal(side, j):
        out = []
        for i in range(2):
            out += [(side, j, i, 0, HALF, 0, HALF, True),
                    (side, j, i, 0, HALF, HALF, HALF, False),
                    (side, j, i, HALF, HALF, HALF, HALF, True)]
        return out

    def full(side, j):
        return [(side, j, i, 0, BLK, q0, HALF, False) for i in range(2) for q0 in (0, HALF)]

    units, done = [], {}
    for p in range(PAIRS):
        c = first + p
        for t in range(nq // 2 - 1):
            on_side0 = t < c
            units += full((p, jnp.where(on_side0, 0, 1)), jnp.where(on_side0, t, t - c))
        units += diagonal((p, 0), c)
        done[len(units)] = (p, 0)
        units += diagonal((p, 1), nq - 1 - c)
        for t in range(nq // 2 - 1, nq - 1):
            units += full((p, 1), t - c)
        done[len(units)] = (p, 1)

    def qk(u):
        side, j, i, k0, nk, q0, nq_ = units[u][:7]
        return lax.dot_general(get_k(i, j, k0, nk), get_q(side, i, q0, nq_), _NT,
                               preferred_element_type=f32)

    bad = jnp.zeros((1, BLK), f32)
    scores = {u: qk(u) for u in range(AHEAD)}
    for u, (side, j, i, k0, nk, q0, nq_, masked) in enumerate(units):
        s = scores.pop(u)
        if masked:
            krow = lax.broadcasted_iota(jnp.int32, s.shape, 0) + k0
            qcol = lax.broadcasted_iota(jnp.int32, s.shape, 1) + q0
            s = jnp.where(krow <= qcol, s, -jnp.inf)
        p = jnp.exp2(s)
        col = jnp.sum(p, axis=0, keepdims=True)
        pb = p.astype(jnp.bfloat16)
        if u + AHEAD < len(units):
            scores[u + AHEAD] = qk(u + AHEAD)
        pv = jnp.dot(vt_ref[0, j, :, k0:k0 + nk], pb, preferred_element_type=f32)
        where = side + (i, slice(None), slice(q0, q0 + nq_))
        l_ref[where] += col
        acc_ref[where] += pv
        if u + 1 in done:
            bad = bad + epilogue(done[u + 1])
    return bad


def _sides():
    return [(p, s) for p in range(PAIRS) for s in range(2)]


def _attn_rescaling(nq, get_q, get_k, vt_ref, acc_ref, l_ref):
    f32 = jnp.float32
    first = pl.program_id(2) * PAIRS
    acc_ref[...] = jnp.zeros(acc_ref.shape, f32)

    for side in _sides():
        n_full = first + side[0] if side[1] == 0 else nq - 1 - first - side[0]

        def step(j, carry, masked, side=side):
            vt = vt_ref[0, j]
            out = []
            for i in range(2):
                m_prev, l_prev = carry[2 * i], carry[2 * i + 1]
                s = lax.dot_general(get_k(i, j, 0, BLK), get_q(side, i, 0, BLK), _NT,
                                    preferred_element_type=f32)
                if masked:
                    krow = lax.broadcasted_iota(jnp.int32, s.shape, 0)
                    qcol = lax.broadcasted_iota(jnp.int32, s.shape, 1)
                    s = jnp.where(krow <= qcol, s, -jnp.inf)
                m_new = jnp.maximum(m_prev, jnp.max(s, axis=0, keepdims=True))
                alpha = jnp.exp2(m_prev - m_new)
                p = jnp.exp2(s - m_new)
                l_new = alpha * l_prev + jnp.sum(p, axis=0, keepdims=True)
                acc_ref[side + (i,)] = alpha * acc_ref[side + (i,)] + jnp.dot(
                    vt, p.astype(jnp.bfloat16), preferred_element_type=f32)
                out += [m_new, l_new]
            return tuple(out)

        neg = jnp.full((1, BLK), -jnp.inf, f32)
        zero = jnp.zeros((1, BLK), f32)
        carry = lax.fori_loop(0, n_full, lambda j, cr: step(j, cr, False),
                              (neg, zero, neg, zero))
        carry = step(n_full, carry, True)
        l_ref[side + (0,)] = carry[1]
        l_ref[side + (1,)] = carry[3]


def _sums_out_of_range(l0, l1):
    ok = ((l0 > SUM_LO) & (l0 < SUM_HI)) & ((l1 > SUM_LO) & (l1 < SUM_HI))
    return jnp.where(ok, 0.0, 1.0)


def _guarded(nq, get_q, get_k, vt_ref, acc_ref, l_ref, epilogue):
    n_bad = jnp.sum(_attn_fast(nq, get_q, get_k, vt_ref, acc_ref, l_ref, epilogue))

    @pl.when(n_bad > 0.0)
    def _():
        _attn_rescaling(nq, get_q, get_k, vt_ref, acc_ref, l_ref)
        for side in _sides():
            epilogue(side)


def _zero(*refs):
    for ref in refs:
        ref[...] = jnp.zeros(ref.shape, ref.dtype)


def _key_rows(j, k0, n):
    return pl.ds(pl.multiple_of(j * BLK + k0, HALF), n)


def _diff_kernel(nq, lam_ref, q_ref, k_ref, vt_ref, g_ref, sg_ref, o_ref, qz_ref, acc_ref, l_ref):
    f32 = jnp.float32
    for p, s in _sides():
        q = q_ref[0, s, p]
        lane = lax.broadcasted_iota(jnp.int32, q.shape, 1)
        zero = jnp.zeros_like(q)
        qz_ref[p, s, 0] = jnp.where(lane < DIFF_HD, q, zero)
        qz_ref[p, s, 1] = jnp.where(lane >= DIFF_HD, q, zero)

    lam_rows = lam_ref[...]
    lam_init = 0.8 - 0.6 * math.exp(-0.3 * 0)
    lam = (jnp.exp(jnp.sum(lam_rows[0:1] * lam_rows[1:2], axis=1, keepdims=True))
           - jnp.exp(jnp.sum(lam_rows[2:3] * lam_rows[3:4], axis=1, keepdims=True))
           + lam_init)

    def epilogue(side):
        p, s = side
        l0, l1 = l_ref[p, s, 0], l_ref[p, s, 1]
        ot = acc_ref[p, s, 0] * (1.0 / l0) - acc_ref[p, s, 1] * (lam / l1)
        ms = jnp.mean(ot * ot, axis=0, keepdims=True)
        ot = ot * lax.rsqrt(ms + RMS_EPS)
        o = ot.T * (sg_ref[...] * (1.0 - lam_init)) * g_ref[0, s, p].astype(f32)
        o_ref[0, s, p] = o.astype(o_ref.dtype)
        return _sums_out_of_range(l0, l1) + jnp.where(ms < OUT_LIMIT, 0.0, 1.0)

    _guarded(nq,
             lambda side, i, q0, n: qz_ref[side[0], side[1], i, q0:q0 + n, :],
             lambda i, j, k0, n: k_ref[0, _key_rows(j, k0, n), :],
             vt_ref, acc_ref, l_ref, epilogue)


def _mla_kernel(nq, q_ref, k_ref, vt_ref, g_ref, o_ref, acc_ref, l_ref):
    f32 = jnp.float32

    def epilogue(side):
        p, s = side
        l0, l1 = l_ref[p, s, 0], l_ref[p, s, 1]
        feat = lax.broadcasted_iota(jnp.int32, (LANES, BLK), 0)
        ot = jnp.where(feat < MLA_V, acc_ref[p, s, 0] * (1.0 / l0), acc_ref[p, s, 1] * (1.0 / l1))
        peak = jnp.max(jnp.abs(ot), axis=0, keepdims=True)
        o_ref[0, s, p] = (ot.T * g_ref[0, s, p].astype(f32)).astype(o_ref.dtype)
        return _sums_out_of_range(l0, l1) + jnp.where(peak < OUT_LIMIT, 0.0, 1.0)

    _guarded(nq,
             lambda side, i, q0, n: q_ref[0, side[1], side[0], q0:q0 + n,
                                          i * LANES:(i + 1) * LANES],
             lambda i, j, k0, n: k_ref[0, _key_rows(j, k0, n), i * LANES:(i + 1) * LANES],
             vt_ref, acc_ref, l_ref, epilogue)


def _attn_scratch():
    return [pltpu.VMEM((PAIRS, 2, 2, LANES, BLK), jnp.float32),
            pltpu.VMEM((PAIRS, 2, 2, 1, BLK), jnp.float32)]


def _pair_spec(width):
    return pl.BlockSpec((1, 2, PAIRS, BLK, width), lambda bi, h, c: (bi, 0, c, 0, h))


def _diff_call(lam_rows, dq, dk, dvt, dg, subg):
    b, _, npair, _, _ = dq.shape
    nq = 2 * npair
    s = nq * BLK
    kspec = pl.BlockSpec((1, s, LANES), lambda bi, h, c: (bi, 0, h))
    vtspec = pl.BlockSpec((1, nq, LANES, BLK), lambda bi, h, c: (bi, 0, h, 0))
    return pl.pallas_call(
        functools.partial(_diff_kernel, nq),
        grid=(b, DIFF_HEADS, npair // PAIRS),
        in_specs=[pl.BlockSpec(lam_rows.shape, lambda bi, h, c: (0, 0)),
                  _pair_spec(LANES), kspec, vtspec, _pair_spec(LANES),
                  pl.BlockSpec(subg.shape, lambda bi, h, c: (0, 0))],
        out_specs=_pair_spec(LANES),
        out_shape=jax.ShapeDtypeStruct(dq.shape, jnp.bfloat16),
        scratch_shapes=[pltpu.VMEM((PAIRS, 2, 2, BLK, LANES), jnp.bfloat16)] + _attn_scratch(),
        compiler_params=pltpu.CompilerParams(
            dimension_semantics=("arbitrary",) * 3, vmem_limit_bytes=VMEM_LIMIT),
        name="diffattn",
    )(lam_rows, dq, dk, dvt, dg, subg)


def _mla_call(mq, mk, mvt, mg):
    b, _, npair, _, _ = mq.shape
    nq = 2 * npair
    s = nq * BLK
    return pl.pallas_call(
        functools.partial(_mla_kernel, nq),
        grid=(b, MLA_HEADS // 2, npair // PAIRS),
        in_specs=[_pair_spec(2 * LANES),
                  pl.BlockSpec((1, s, 2 * LANES), lambda bi, h, c: (bi, 0, h)),
                  pl.BlockSpec((1, nq, LANES, BLK), lambda bi, h, c: (bi, 0, h, 0)),
                  _pair_spec(LANES)],
        out_specs=_pair_spec(LANES),
        out_shape=jax.ShapeDtypeStruct(mg.shape, jnp.bfloat16),
        scratch_shapes=_attn_scratch(),
        compiler_params=pltpu.CompilerParams(
            dimension_semantics=("arbitrary",) * 3, vmem_limit_bytes=VMEM_LIMIT),
        name="mlaattn",
    )(mq, mk, mvt, mg)


def _out_kernel(x_ref, od_ref, om_ref, p_ref, wo_ref, wp_ref, wg_ref, fg_ref, o_ref):
    f32 = jnp.float32
    bf16 = jnp.bfloat16
    for r in (slice(0, HALF), slice(HALF, BLK)):
        h = (x_ref[r, :]
             + jnp.dot(od_ref[0, r, :], wo_ref[0:DIFF_WIDTH, :], preferred_element_type=f32)
             + jnp.dot(om_ref[0, r, :], wo_ref[DIFF_WIDTH:, :], preferred_element_type=f32))
        gate = jax.nn.sigmoid(jnp.dot(h.astype(bf16), wg_ref[...], preferred_element_type=f32))
        emb = jnp.dot(p_ref[r, :].astype(bf16), wp_ref[...], preferred_element_type=f32)
        h = h + emb * gate
        o_ref[r, :] = _rms(h, fg_ref[...])


def _out_call(x2, od, om, p2, wo, wp, wg, fg, nq):
    t = x2.shape[0]
    row = lambda w: pl.BlockSpec((BLK, w), lambda i: (i, 0))
    full = lambda a: pl.BlockSpec(a.shape, lambda i: (0, 0))
    fold = lambda w: pl.BlockSpec((1, BLK, w), lambda i: (i // nq * nq + _folded(i % nq, nq), 0, 0))
    return pl.pallas_call(
        _out_kernel,
        grid=(t // BLK,),
        in_specs=[row(D_MODEL), fold(DIFF_WIDTH), fold(MLA_WIDTH), row(PLE_DIM),
                  full(wo), full(wp), full(wg), full(fg)],
        out_specs=row(D_MODEL),
        out_shape=jax.ShapeDtypeStruct((t, D_MODEL), jnp.float32),
        compiler_params=pltpu.CompilerParams(
            dimension_semantics=("arbitrary",), vmem_limit_bytes=VMEM_LIMIT),
        name="outproj",
    )(x2, od, om, p2, wo, wp, wg, fg)


def _rope_tables():
    f32 = jnp.float32
    lane = jnp.arange(LANES)
    inv_d = ROPE_THETA ** (-jnp.arange(0, DIFF_HD, 2, dtype=f32) / DIFF_HD)
    half_d = DIFF_HD // 2
    inv_d_l = inv_d[lane % half_d]
    sp_d = jnp.where(lane % DIFF_HD >= half_d, 1.0, 0.0)
    sm_d = jnp.where(lane % DIFF_HD < half_d, -1.0, 0.0)
    inv_m = ROPE_THETA ** (-jnp.arange(0, MLA_ROPE, 2, dtype=f32) / MLA_ROPE)
    half_m = MLA_ROPE // 2
    in_rope = (lane >= MLA_NOPE) & (lane < MLA_NOPE + MLA_ROPE)
    inv_m_l = jnp.where(in_rope, inv_m[(lane - MLA_NOPE) % half_m], 0.0)
    sp_m = jnp.where(in_rope & (lane >= MLA_NOPE + half_m), 1.0, 0.0)
    sm_m = jnp.where(in_rope & (lane < MLA_NOPE + half_m), -1.0, 0.0)
    zero = jnp.zeros((LANES,), f32)
    return jnp.stack([inv_d_l, sp_d + sm_d, sp_d, inv_m_l, sp_m + sm_m, sp_m, zero, zero]).astype(f32)


def kernel(x, p, positions, norm_g, w_in, diff_lambda, diff_subln_g, mla_q_norm_g, w_uq,
           mla_kv_norm_g, w_ukv, w_out, w_ple, w_ple_gate, final_norm_g):
    b, s, d = x.shape
    t = b * s
    nq = s // BLK
    assert s % (2 * PAIRS * BLK) == 0
    f32 = jnp.float32
    bf16 = jnp.bfloat16
    i = 0

    offs = [0, 512, 1024, 1536, 2048, 2432, 2560, 2592, 3104]
    seg = [w_in[i][:, offs[j]:offs[j + 1]] for j in range(8)]
    wdq, wdk, wdv, wdg, wcq, wckv, wkr, wmg = seg
    zc = lambda n: jnp.zeros((d, n), f32)
    w_in_p = jnp.concatenate(
        [wdq, wdk, wdg, wmg, wcq, wckv, zc(MLA_NOPE), wkr, zc(LANES - MLA_NOPE - MLA_ROPE)],
        axis=1).astype(bf16)
    wuq_p = jnp.pad(w_uq[i].reshape(MLA_Q_LORA, MLA_HEADS, MLA_NOPE + MLA_ROPE),
                    ((0, 0), (0, 0), (0, LANES - MLA_NOPE - MLA_ROPE))
                    ).reshape(MLA_Q_LORA, MLA_HEADS * LANES).astype(bf16)
    wkv3 = w_ukv[i].reshape(MLA_KV_LORA, MLA_HEADS, MLA_NOPE + MLA_V)
    wk_p = jnp.pad(wkv3[:, :, :MLA_NOPE], ((0, 0), (0, 0), (0, LANES - MLA_NOPE))
                   ).reshape(MLA_KV_LORA, MLA_HEADS * LANES).astype(bf16)
    wvt_p = wkv3[:, :, MLA_NOPE:].reshape(MLA_KV_LORA, MLA_WIDTH).T.astype(bf16)
    wdvt = wdv.T.astype(bf16)

    x2 = x.reshape(t, d)
    posb = positions.astype(f32).reshape(t // BLK, 1, BLK)
    dq, dk, dvt, dg, mq, mk, mvt, mg = _proj_call(
        x2, posb, _rope_tables(), norm_g[i].reshape(1, d), w_in_p,
        mla_q_norm_g[i].reshape(1, MLA_Q_LORA), wuq_p,
        mla_kv_norm_g[i].reshape(1, MLA_KV_LORA), wk_p, wvt_p, wdvt, nq)

    r3 = lambda a: a.reshape(b, s, a.shape[-1])
    r4 = lambda a: a.reshape(b, nq, a.shape[-2], BLK)
    r5 = lambda a: a.reshape(b, 2, nq // 2, BLK, a.shape[-1])
    od = _diff_call(diff_lambda[i].astype(f32), r5(dq), r3(dk), r4(dvt), r5(dg),
                    diff_subln_g[i].reshape(1, 2 * DIFF_HD))
    om = _mla_call(r5(mq), r3(mk), r4(mvt), r5(mg))

    out = _out_call(x2, od.reshape(t // BLK, BLK, DIFF_WIDTH), om.reshape(t // BLK, BLK, MLA_WIDTH),
                    p[i].reshape(t, PLE_DIM), w_out[i].astype(bf16), w_ple[i].astype(bf16),
                    w_ple_gate[i].astype(bf16), final_norm_g.reshape(1, d), nq)
    return out.reshape(b, s, d)
```

```python
import functools
import math

import jax
import jax.numpy as jnp
from jax import lax
from jax.experimental import pallas as pl
from jax.experimental.pallas import tpu as pltpu

D_MODEL = 1024
PLE_DIM = 256
ROPE_THETA = 10000.0
RMS_EPS = 1e-6

DIFF_WIDTH = 512
DIFF_HEADS = 4
DIFF_HD = 64
MLA_WIDTH = 512
MLA_HEADS = 8
MLA_NOPE = 64
MLA_ROPE = 32
MLA_V = 64
MLA_Q_LORA = 384
MLA_KV_LORA = 128

LANES = 128
LOG2E = math.log2(math.e)
VMEM_LIMIT = 48 * 1024 * 1024

O_DQ, O_DK, O_DG, O_MG = 0, 512, 1024, 1536
O_CQ = 2048
O_CKV = O_CQ + MLA_Q_LORA
O_KR = O_CKV + MLA_KV_LORA
D_IN_P = O_KR + LANES

_NT = (((1,), (1,)), ((), ()))
BLK = 512
HALF = BLK // 2
PAIRS = 4
AHEAD = 3


def _rms(x, g):
    return x * lax.rsqrt(jnp.mean(x * x, axis=-1, keepdims=True) + RMS_EPS) * g


def _silu(x):
    return x / (1.0 + jnp.exp(-x))


def _rope128(x, c, s, below, half):
    partner = jnp.where(below, pltpu.roll(x, half, axis=1), pltpu.roll(x, LANES - half, axis=1))
    return x * c + partner * s


def _folded(c, n):
    return jnp.where(c < n // 2, c, n + n // 2 - 1 - c)


def _proj_kernel(x_ref, pos_ref, posprev_ref, tab_ref, ng_ref, w_ref, qg_ref, wuq_ref, kvg_ref,
                 wk_ref, wvt_ref, wdvt_ref,
                 dq_ref, dk_ref, dvt_ref, dg_ref, mq_ref, mk_ref, mvt_ref, mg_ref,
                 rowtab_ref, rope_ref, nb_ref, flag_ref):
    f32 = jnp.float32
    bf16 = jnp.bfloat16
    i = pl.program_id(0)
    row = lax.broadcasted_iota(jnp.int32, (BLK, LANES), 0).astype(f32)

    def store_tables(slot, t, c, s):
        rope_ref[slot, 2 * t] = c
        rope_ref[slot, 2 * t + 1] = s * tab_ref[3 * t + 1:3 * t + 2, :]

    def prep(slot):
        nb_ref[slot] = _rms(x_ref[...], ng_ref[...]).astype(bf16)
        pos = pos_ref[0]
        base = pos[:, 0:1]
        idx = lax.broadcasted_iota(jnp.int32, (1, BLK), 1).astype(f32)
        flag_ref[slot] = jnp.sum(jnp.where(pos == base + idx, 0.0, 1.0))
        for t in range(2):
            ang = base * tab_ref[3 * t:3 * t + 1, :]
            cb, sb = jnp.cos(ang), jnp.sin(ang)
            cr, sr = rowtab_ref[2 * t], rowtab_ref[2 * t + 1]
            store_tables(slot, t, cb * cr - sb * sr, sb * cr + cb * sr)

    def direct_tables(slot):
        pos = jnp.broadcast_to(posprev_ref[0], (LANES, BLK)).T
        for t in range(2):
            ang = pos * tab_ref[3 * t:3 * t + 1, :]
            store_tables(slot, t, jnp.cos(ang), jnp.sin(ang))

    def main(slot):
        nb = nb_ref[slot]
        cd, sd, below_d = rope_ref[slot, 0], rope_ref[slot, 1], tab_ref[2:3, :] > 0.0
        cm, sm, below_m = rope_ref[slot, 2], rope_ref[slot, 3], tab_ref[5:6, :] > 0.0

        def proj(off, width):
            return jnp.dot(nb, w_ref[:, off:off + width], preferred_element_type=f32)

        lat = proj(O_CQ, D_IN_P - O_CQ)
        q_scale = DIFF_HD ** -0.5 * LOG2E
        dq = proj(O_DQ, DIFF_WIDTH)
        dk = proj(O_DK, DIFF_WIDTH)
        cqn = _rms(lat[:, :MLA_Q_LORA], qg_ref[...]).astype(bf16)
        ckvn = _rms(lat[:, O_CKV - O_CQ:O_KR - O_CQ], kvg_ref[...]).astype(bf16)
        for c in range(DIFF_WIDTH // LANES):
            sl = slice(c * LANES, (c + 1) * LANES)
            dq_ref[0, :, sl] = (_rope128(dq[:, sl], cd, sd, below_d, DIFF_HD // 2)
                                * q_scale).astype(bf16)
            dk_ref[:, sl] = _rope128(dk[:, sl], cd, sd, below_d, DIFF_HD // 2).astype(bf16)

        mq = jnp.dot(cqn, wuq_ref[...], preferred_element_type=f32)
        m_scale = (MLA_NOPE + MLA_ROPE) ** -0.5 * LOG2E
        kn = jnp.dot(ckvn, wk_ref[...], preferred_element_type=f32)
        kr = _rope128(lat[:, O_KR - O_CQ:], cm, sm, below_m, MLA_ROPE // 2)
        for h in range(MLA_HEADS):
            sl = slice(h * LANES, (h + 1) * LANES)
            mq_ref[0, :, sl] = (_rope128(mq[:, sl], cm, sm, below_m, MLA_ROPE // 2)
                                * m_scale).astype(bf16)
            mk_ref[:, sl] = (kn[:, sl] + kr).astype(bf16)

        dg_ref[0] = _silu(proj(O_DG, DIFF_WIDTH)).astype(bf16)
        mg_ref[0] = _silu(proj(O_MG, MLA_WIDTH)).astype(bf16)
        mvt_ref[0] = lax.dot_general(wvt_ref[...], ckvn, _NT,
                                     preferred_element_type=f32).astype(bf16)
        dvt_ref[0] = lax.dot_general(wdvt_ref[...], nb, _NT,
                                     preferred_element_type=f32).astype(bf16)

    def step(cur):
        prv = 1 - cur

        @pl.when(flag_ref[prv] != 0.0)
        def _():
            direct_tables(prv)

        main(prv)
        prep(cur)

    @pl.when(i == 0)
    def _():
        for t in range(2):
            ang = row * tab_ref[3 * t:3 * t + 1, :]
            rowtab_ref[2 * t] = jnp.cos(ang)
            rowtab_ref[2 * t + 1] = jnp.sin(ang)
        prep(0)

    @pl.when(i % 2 == 1)
    def _():
        step(1)

    @pl.when((i > 0) & (i % 2 == 0))
    def _():
        step(0)


def _proj_call(x2, posb, tab, ng, w_in_p, qg, wuq_p, kvg, wk_p, wvt_p, wdvt, nq):
    t = x2.shape[0]
    n = t // BLK
    bf16 = jnp.bfloat16
    nxt = lambda i: jnp.minimum(i, n - 1)
    cur = lambda i: jnp.maximum(i - 1, 0)
    row_in = lambda w, blk: pl.BlockSpec((BLK, w), lambda i: (blk(i), 0))
    pos_in = lambda blk: pl.BlockSpec((1, 1, BLK), lambda i: (blk(i), 0, 0))
    full = lambda a: pl.BlockSpec(a.shape, lambda i: (0, 0))
    row = lambda w: pl.BlockSpec((BLK, w), lambda i: (cur(i), 0))
    blk3 = lambda w: pl.BlockSpec((1, w, BLK), lambda i: (cur(i), 0, 0))
    fold = lambda w: pl.BlockSpec(
        (1, BLK, w), lambda i: (cur(i) // nq * nq + _folded(cur(i) % nq, nq), 0, 0))
    rows = lambda w: jax.ShapeDtypeStruct((t, w), bf16)
    blks = lambda w: jax.ShapeDtypeStruct((n, w, BLK), bf16)
    folds = lambda w: jax.ShapeDtypeStruct((n, BLK, w), bf16)
    return pl.pallas_call(
        _proj_kernel,
        grid=(n + 1,),
        in_specs=[row_in(D_MODEL, nxt), pos_in(nxt), pos_in(cur), full(tab),
                  full(ng), full(w_in_p), full(qg), full(wuq_p), full(kvg), full(wk_p),
                  full(wvt_p), full(wdvt)],
        out_specs=[fold(DIFF_WIDTH), row(DIFF_WIDTH), blk3(DIFF_WIDTH), fold(DIFF_WIDTH),
                   fold(MLA_HEADS * LANES), row(MLA_HEADS * LANES), blk3(MLA_WIDTH),
                   fold(MLA_WIDTH)],
        out_shape=[folds(DIFF_WIDTH), rows(DIFF_WIDTH), blks(DIFF_WIDTH), folds(DIFF_WIDTH),
                   folds(MLA_HEADS * LANES), rows(MLA_HEADS * LANES), blks(MLA_WIDTH),
                   folds(MLA_WIDTH)],
        scratch_shapes=[pltpu.VMEM((4, BLK, LANES), jnp.float32),
                        pltpu.VMEM((2, 4, BLK, LANES), jnp.float32),
                        pltpu.VMEM((2, BLK, D_MODEL), bf16),
                        pltpu.SMEM((2,), jnp.float32)],
        compiler_params=pltpu.CompilerParams(
            dimension_semantics=("arbitrary",), vmem_limit_bytes=VMEM_LIMIT),
        name="proj",
    )(x2, posb, posb, tab, ng, w_in_p, qg, wuq_p, kvg, wk_p, wvt_p, wdvt)


SUM_LO, SUM_HI = 2.0 ** -64, 2.0 ** 64
OUT_LIMIT = 2.0 ** 100


def _attn_fast(nq, get_q, get_k, vt_ref, acc_ref, l_ref, epilogue):
    f32 = jnp.float32
    first = pl.program_id(2) * PAIRS

    _zero(acc_ref, l_ref)

    def diagonal(side, j):
        out = []
        for i in range(2):
            out += [(side, j, i, 0, HALF, 0, HALF, True),
                    (side, j, i, 0, HALF, HALF, HALF, False),
                    (side, j, i, HALF, HALF, HALF, HALF, True)]
        return out

    def full(side, j):
        return [(side, j, i, 0, BLK, q0, HALF, False) for i in range(2) for q0 in (0, HALF)]

    units, done = [], {}
    for p in range(PAIRS):
        c = first + p
        for t in range(nq // 2 - 1):
            on_side0 = t < c
            units += full((p, jnp.where(on_side0, 0, 1)), jnp.where(on_side0, t, t - c))
        units += diagonal((p, 0), c)
        done[len(units)] = (p, 0)
        units += diagonal((p, 1), nq - 1 - c)
        for t in range(nq // 2 - 1, nq - 1):
            units += full((p, 1), t - c)
        done[len(units)] = (p, 1)

    def qk(u):
        side, j, i, k0, nk, q0, nq_ = units[u][:7]
        return lax.dot_general(get_k(i, j, k0, nk), get_q(side, i, q0, nq_), _NT,
                               preferred_element_type=f32)

    bad = jnp.zeros((1, BLK), f32)
    scores = {u: qk(u) for u in range(AHEAD)}
    for u, (side, j, i, k0, nk, q0, nq_, masked) in enumerate(units):
        s = scores.pop(u)
        if masked:
            krow = lax.broadcasted_iota(jnp.int32, s.shape, 0) + k0
            qcol = lax.broadcasted_iota(jnp.int32, s.shape, 1) + q0
            s = jnp.where(krow <= qcol, s, -jnp.inf)
        p = jnp.exp2(s)
        col = jnp.sum(p, axis=0, keepdims=True)
        pb = p.astype(jnp.bfloat16)
        if u + AHEAD < len(units):
            scores[u + AHEAD] = qk(u + AHEAD)
        pv = jnp.dot(vt_ref[0, j, :, k0:k0 + nk], pb, preferred_element_type=f32)
        where = side + (i, slice(None), slice(q0, q0 + nq_))
        l_ref[where] += col
        acc_ref[where] += pv
        if u + 1 in done:
            bad = bad + epilogue(done[u + 1])
    return bad


def _sides():
    return [(p, s) for p in range(PAIRS) for s in range(2)]


def _attn_rescaling(nq, get_q, get_k, vt_ref, acc_ref, l_ref):
    f32 = jnp.float32
    first = pl.program_id(2) * PAIRS
    acc_ref[...] = jnp.zeros(acc_ref.shape, f32)

    for side in _sides():
        n_full = first + side[0] if side[1] == 0 else nq - 1 - first - side[0]

        def step(j, carry, masked, side=side):
            vt = vt_ref[0, j]
            out = []
            for i in range(2):
                m_prev, l_prev = carry[2 * i], carry[2 * i + 1]
                s = lax.dot_general(get_k(i, j, 0, BLK), get_q(side, i, 0, BLK), _NT,
                                    preferred_element_type=f32)
                if masked:
                    krow = lax.broadcasted_iota(jnp.int32, s.shape, 0)
                    qcol = lax.broadcasted_iota(jnp.int32, s.shape, 1)
                    s = jnp.where(krow <= qcol, s, -jnp.inf)
                m_new = jnp.maximum(m_prev, jnp.max(s, axis=0, keepdims=True))
                alpha = jnp.exp2(m_prev - m_new)
                p = jnp.exp2(s - m_new)
                l_new = alpha * l_prev + jnp.sum(p, axis=0, keepdims=True)
                acc_ref[side + (i,)] = alpha * acc_ref[side + (i,)] + jnp.dot(
                    vt, p.astype(jnp.bfloat16), preferred_element_type=f32)
                out += [m_new, l_new]
            return tuple(out)

        neg = jnp.full((1, BLK), -jnp.inf, f32)
        zero = jnp.zeros((1, BLK), f32)
        carry = lax.fori_loop(0, n_full, lambda j, cr: step(j, cr, False),
                              (neg, zero, neg, zero))
        carry = step(n_full, carry, True)
        l_ref[side + (0,)] = carry[1]
        l_ref[side + (1,)] = carry[3]


def _sums_out_of_range(l0, l1):
    ok = ((l0 > SUM_LO) & (l0 < SUM_HI)) & ((l1 > SUM_LO) & (l1 < SUM_HI))
    return jnp.where(ok, 0.0, 1.0)


def _guarded(nq, get_q, get_k, vt_ref, acc_ref, l_ref, epilogue):
    n_bad = jnp.sum(_attn_fast(nq, get_q, get_k, vt_ref, acc_ref, l_ref, epilogue))

    @pl.when(n_bad > 0.0)
    def _():
        _attn_rescaling(nq, get_q, get_k, vt_ref, acc_ref, l_ref)
        for side in _sides():
            epilogue(side)


def _zero(*refs):
    for ref in refs:
        ref[...] = jnp.zeros(ref.shape, ref.dtype)


def _key_rows(j, k0, n):
    return pl.ds(pl.multiple_of(j * BLK + k0, HALF), n)


def _diff_kernel(nq, lam_ref, q_ref, k_ref, vt_ref, g_ref, sg_ref, o_ref, qz_ref, acc_ref, l_ref):
    f32 = jnp.float32
    for p, s in _sides():
        q = q_ref[0, s, p]
        lane = lax.broadcasted_iota(jnp.int32, q.shape, 1)
        zero = jnp.zeros_like(q)
        qz_ref[p, s, 0] = jnp.where(lane < DIFF_HD, q, zero)
        qz_ref[p, s, 1] = jnp.where(lane >= DIFF_HD, q, zero)

    lam_rows = lam_ref[...]
    lam_init = 0.8 - 0.6 * math.exp(-0.3 * 0)
    lam = (jnp.exp(jnp.sum(lam_rows[0:1] * lam_rows[1:2], axis=1, keepdims=True))
           - jnp.exp(jnp.sum(lam_rows[2:3] * lam_rows[3:4], axis=1, keepdims=True))
           + lam_init)

    def epilogue(side):
        p, s = side
        l0, l1 = l_ref[p, s, 0], l_ref[p, s, 1]
        ot = acc_ref[p, s, 0] * (1.0 / l0) - acc_ref[p, s, 1] * (lam / l1)
        ms = jnp.mean(ot * ot, axis=0, keepdims=True)
        ot = ot * lax.rsqrt(ms + RMS_EPS)
        o = ot.T * (sg_ref[...] * (1.0 - lam_init)) * g_ref[0, s, p].astype(f32)
        o_ref[0, s, p] = o.astype(o_ref.dtype)
        return _sums_out_of_range(l0, l1) + jnp.where(ms < OUT_LIMIT, 0.0, 1.0)

    _guarded(nq,
             lambda side, i, q0, n: qz_ref[side[0], side[1], i, q0:q0 + n, :],
             lambda i, j, k0, n: k_ref[0, _key_rows(j, k0, n), :],
             vt_ref, acc_ref, l_ref, epilogue)


def _mla_kernel(nq, q_ref, k_ref, vt_ref, g_ref, o_ref, acc_ref, l_ref):
    f32 = jnp.float32

    def epilogue(side):
        p, s = side
        l0, l1 = l_ref[p, s, 0], l_ref[p, s, 1]
        feat = lax.broadcasted_iota(jnp.int32, (LANES, BLK), 0)
        ot = jnp.where(feat < MLA_V, acc_ref[p, s, 0] * (1.0 / l0), acc_ref[p, s, 1] * (1.0 / l1))
        peak = jnp.max(jnp.abs(ot), axis=0, keepdims=True)
        o_ref[0, s, p] = (ot.T * g_ref[0, s, p].astype(f32)).astype(o_ref.dtype)
        return _sums_out_of_range(l0, l1) + jnp.where(peak < OUT_LIMIT, 0.0, 1.0)

    _guarded(nq,
             lambda side, i, q0, n: q_ref[0, side[1], side[0], q0:q0 + n,
                                          i * LANES:(i + 1) * LANES],
             lambda i, j, k0, n: k_ref[0, _key_rows(j, k0, n), i * LANES:(i + 1) * LANES],
             vt_ref, acc_ref, l_ref, epilogue)


def _attn_scratch():
    return [pltpu.VMEM((PAIRS, 2, 2, LANES, BLK), jnp.float32),
            pltpu.VMEM((PAIRS, 2, 2, 1, BLK), jnp.float32)]


def _pair_spec(width):
    return pl.BlockSpec((1, 2, PAIRS, BLK, width), lambda bi, h, c: (bi, 0, c, 0, h))


def _diff_call(lam_rows, dq, dk, dvt, dg, subg):
    b, _, npair, _, _ = dq.shape
    nq = 2 * npair
    s = nq * BLK
    kspec = pl.BlockSpec((1, s, LANES), lambda bi, h, c: (bi, 0, h))
    vtspec = pl.BlockSpec((1, nq, LANES, BLK), lambda bi, h, c: (bi, 0, h, 0))
    return pl.pallas_call(
        functools.partial(_diff_kernel, nq),
        grid=(b, DIFF_HEADS, npair // PAIRS),
        in_specs=[pl.BlockSpec(lam_rows.shape, lambda bi, h, c: (0, 0)),
                  _pair_spec(LANES), kspec, vtspec, _pair_spec(LANES),
                  pl.BlockSpec(subg.shape, lambda bi, h, c: (0, 0))],
        out_specs=_pair_spec(LANES),
        out_shape=jax.ShapeDtypeStruct(dq.shape, jnp.bfloat16),
        scratch_shapes=[pltpu.VMEM((PAIRS, 2, 2, BLK, LANES), jnp.bfloat16)] + _attn_scratch(),
        compiler_params=pltpu.CompilerParams(
            dimension_semantics=("arbitrary",) * 3, vmem_limit_bytes=VMEM_LIMIT),
        name="diffattn",
    )(lam_rows, dq, dk, dvt, dg, subg)


def _mla_call(mq, mk, mvt, mg):
    b, _, npair, _, _ = mq.shape
    nq = 2 * npair
    s = nq * BLK
    return pl.pallas_call(
        functools.partial(_mla_kernel, nq),
        grid=(b, MLA_HEADS // 2, npair // PAIRS),
        in_specs=[_pair_spec(2 * LANES),
                  pl.BlockSpec((1, s, 2 * LANES), lambda bi, h, c: (bi, 0, h)),
                  pl.BlockSpec((1, nq, LANES, BLK), lambda bi, h, c: (bi, 0, h, 0)),
                  _pair_spec(LANES)],
        out_specs=_pair_spec(LANES),
        out_shape=jax.ShapeDtypeStruct(mg.shape, jnp.bfloat16),
        scratch_shapes=_attn_scratch(),
        compiler_params=pltpu.CompilerParams(
            dimension_semantics=("arbitrary",) * 3, vmem_limit_bytes=VMEM_LIMIT),
        name="mlaattn",
    )(mq, mk, mvt, mg)


def _out_kernel(nq, x_ref, od_ref, om_ref, p_ref, wo_ref, wp_ref, wg_ref, fg_ref, o_ref):
    f32 = jnp.float32
    bf16 = jnp.bfloat16
    reverse = (2 * pl.program_id(0)) % nq >= nq // 2
    for b in range(2):
        src = jnp.where(reverse, 1 - b, b)
        for r0 in (0, HALF):
            r = slice(r0, r0 + HALF)
            rows = slice(b * BLK + r0, b * BLK + r0 + HALF)
            h = (x_ref[rows, :]
                 + jnp.dot(od_ref[src, r, :], wo_ref[0:DIFF_WIDTH, :], preferred_element_type=f32)
                 + jnp.dot(om_ref[src, r, :], wo_ref[DIFF_WIDTH:, :], preferred_element_type=f32))
            gate = jax.nn.sigmoid(jnp.dot(h.astype(bf16), wg_ref[...],
                                          preferred_element_type=f32))
            emb = jnp.dot(p_ref[rows, :].astype(bf16), wp_ref[...], preferred_element_type=f32)
            h = h + emb * gate
            o_ref[rows, :] = _rms(h, fg_ref[...])


def _out_call(x2, od, om, p2, wo, wp, wg, fg, nq):
    t = x2.shape[0]
    row = lambda w: pl.BlockSpec((2 * BLK, w), lambda i: (i, 0))
    full = lambda a: pl.BlockSpec(a.shape, lambda i: (0, 0))

    def pair_of_blocks(i):
        c = (2 * i) % nq
        return (2 * i) // nq * (nq // 2) + jnp.where(c < nq // 2, c, nq + nq // 2 - 2 - c) // 2

    fold = lambda w: pl.BlockSpec((2, BLK, w), lambda i: (pair_of_blocks(i), 0, 0))
    return pl.pallas_call(
        functools.partial(_out_kernel, nq),
        grid=(t // (2 * BLK),),
        in_specs=[row(D_MODEL), fold(DIFF_WIDTH), fold(MLA_WIDTH), row(PLE_DIM),
                  full(wo), full(wp), full(wg), full(fg)],
        out_specs=row(D_MODEL),
        out_shape=jax.ShapeDtypeStruct((t, D_MODEL), jnp.float32),
        compiler_params=pltpu.CompilerParams(
            dimension_semantics=("arbitrary",), vmem_limit_bytes=VMEM_LIMIT),
        name="outproj",
    )(x2, od, om, p2, wo, wp, wg, fg)


def _rope_tables():
    f32 = jnp.float32
    lane = jnp.arange(LANES)
    inv_d = ROPE_THETA ** (-jnp.arange(0, DIFF_HD, 2, dtype=f32) / DIFF_HD)
    half_d = DIFF_HD // 2
    inv_d_l = inv_d[lane % half_d]
    sp_d = jnp.where(lane % DIFF_HD >= half_d, 1.0, 0.0)
    sm_d = jnp.where(lane % DIFF_HD < half_d, -1.0, 0.0)
    inv_m = ROPE_THETA ** (-jnp.arange(0, MLA_ROPE, 2, dtype=f32) / MLA_ROPE)
    half_m = MLA_ROPE // 2
    in_rope = (lane >= MLA_NOPE) & (lane < MLA_NOPE + MLA_ROPE)
    inv_m_l = jnp.where(in_rope, inv_m[(lane - MLA_NOPE) % half_m], 0.0)
    sp_m = jnp.where(in_rope & (lane >= MLA_NOPE + half_m), 1.0, 0.0)
    sm_m = jnp.where(in_rope & (lane < MLA_NOPE + half_m), -1.0, 0.0)
    zero = jnp.zeros((LANES,), f32)
    return jnp.stack([inv_d_l, sp_d + sm_d, sp_d, inv_m_l, sp_m + sm_m, sp_m, zero, zero]).astype(f32)


def kernel(x, p, positions, norm_g, w_in, diff_lambda, diff_subln_g, mla_q_norm_g, w_uq,
           mla_kv_norm_g, w_ukv, w_out, w_ple, w_ple_gate, final_norm_g):
    b, s, d = x.shape
    t = b * s
    nq = s // BLK
    assert s % (2 * PAIRS * BLK) == 0
    f32 = jnp.float32
    bf16 = jnp.bfloat16
    i = 0

    offs = [0, 512, 1024, 1536, 2048, 2432, 2560, 2592, 3104]
    seg = [w_in[i][:, offs[j]:offs[j + 1]] for j in range(8)]
    wdq, wdk, wdv, wdg, wcq, wckv, wkr, wmg = seg
    zc = lambda n: jnp.zeros((d, n), f32)
    w_in_p = jnp.concatenate(
        [wdq, wdk, wdg, wmg, wcq, wckv, zc(MLA_NOPE), wkr, zc(LANES - MLA_NOPE - MLA_ROPE)],
        axis=1).astype(bf16)
    wuq_p = jnp.pad(w_uq[i].reshape(MLA_Q_LORA, MLA_HEADS, MLA_NOPE + MLA_ROPE),
                    ((0, 0), (0, 0), (0, LANES - MLA_NOPE - MLA_ROPE))
                    ).reshape(MLA_Q_LORA, MLA_HEADS * LANES).astype(bf16)
    wkv3 = w_ukv[i].reshape(MLA_KV_LORA, MLA_HEADS, MLA_NOPE + MLA_V)
    wk_p = jnp.pad(wkv3[:, :, :MLA_NOPE], ((0, 0), (0, 0), (0, LANES - MLA_NOPE))
                   ).reshape(MLA_KV_LORA, MLA_HEADS * LANES).astype(bf16)
    wvt_p = wkv3[:, :, MLA_NOPE:].reshape(MLA_KV_LORA, MLA_WIDTH).T.astype(bf16)
    wdvt = wdv.T.astype(bf16)

    x2 = x.reshape(t, d)
    posb = positions.astype(f32).reshape(t // BLK, 1, BLK)
    dq, dk, dvt, dg, mq, mk, mvt, mg = _proj_call(
        x2, posb, _rope_tables(), norm_g[i].reshape(1, d), w_in_p,
        mla_q_norm_g[i].reshape(1, MLA_Q_LORA), wuq_p,
        mla_kv_norm_g[i].reshape(1, MLA_KV_LORA), wk_p, wvt_p, wdvt, nq)

    r3 = lambda a: a.reshape(b, s, a.shape[-1])
    r4 = lambda a: a.reshape(b, nq, a.shape[-2], BLK)
    r5 = lambda a: a.reshape(b, 2, nq // 2, BLK, a.shape[-1])
    od = _diff_call(diff_lambda[i].astype(f32), r5(dq), r3(dk), r4(dvt), r5(dg),
                    diff_subln_g[i].reshape(1, 2 * DIFF_HD))
    om = _mla_call(r5(mq), r3(mk), r4(mvt), r5(mg))

    out = _out_call(x2, od.reshape(t // BLK, BLK, DIFF_WIDTH), om.reshape(t // BLK, BLK, MLA_WIDTH),
                    p[i].reshape(t, PLE_DIM), w_out[i].astype(bf16), w_ple[i].astype(bf16),
                    w_ple_gate[i].astype(bf16), final_norm_g.reshape(1, d), nq)
    return out.reshape(b, s, d)
```

```python
import functools
import math

import jax
import jax.numpy as jnp
from jax import lax
from jax.experimental import pallas as pl
from jax.experimental.pallas import tpu as pltpu

D_MODEL = 1024
PLE_DIM = 256
ROPE_THETA = 10000.0
RMS_EPS = 1e-6

DIFF_WIDTH = 512
DIFF_HEADS = 4
DIFF_HD = 64
MLA_WIDTH = 512
MLA_HEADS = 8
MLA_NOPE = 64
MLA_ROPE = 32
MLA_V = 64
MLA_Q_LORA = 384
MLA_KV_LORA = 128

LANES = 128
LOG2E = math.log2(math.e)
VMEM_LIMIT = 48 * 1024 * 1024

O_DQ, O_DK, O_DG, O_MG = 0, 512, 1024, 1536
O_CQ = 2048
O_CKV = O_CQ + MLA_Q_LORA
O_KR = O_CKV + MLA_KV_LORA
D_IN_P = O_KR + LANES

_NT = (((1,), (1,)), ((), ()))
BLK = 512
HALF = BLK // 2
PAIRS = 4
AHEAD = 3


def _rms(x, g):
    return x * lax.rsqrt(jnp.mean(x * x, axis=-1, keepdims=True) + RMS_EPS) * g


def _silu(x):
    return x / (1.0 + jnp.exp(-x))


def _rope128(x, c, s, below, half):
    partner = jnp.where(below, pltpu.roll(x, half, axis=1), pltpu.roll(x, LANES - half, axis=1))
    return x * c + partner * s


def _folded(c, n):
    return jnp.where(c < n // 2, c, n + n // 2 - 1 - c)


def _proj_kernel(x_ref, pos_ref, posprev_ref, tab_ref, ng_ref, w_ref, qg_ref, wuq_ref, kvg_ref,
                 wk_ref, wvt_ref, wdvt_ref,
                 dq_ref, dk_ref, dvt_ref, dg_ref, mq_ref, mk_ref, mvt_ref, mg_ref,
                 rowtab_ref, rope_ref, nb_ref, flag_ref):
    f32 = jnp.float32
    bf16 = jnp.bfloat16
    i = pl.program_id(0)
    row = lax.broadcasted_iota(jnp.int32, (BLK, LANES), 0).astype(f32)

    def store_tables(slot, t, c, s):
        rope_ref[slot, 2 * t] = c
        rope_ref[slot, 2 * t + 1] = s * tab_ref[3 * t + 1:3 * t + 2, :]

    def prep(slot, after=None):
        x = x_ref[...]
        if after is not None:
            x = x + jnp.minimum(jnp.abs(after), 0.0)
        nb_ref[slot] = _rms(x, ng_ref[...]).astype(bf16)
        pos = pos_ref[0]
        base = pos[:, 0:1]
        idx = lax.broadcasted_iota(jnp.int32, (1, BLK), 1).astype(f32)
        flag_ref[slot] = jnp.sum(jnp.where(pos == base + idx, 0.0, 1.0))
        for t in range(2):
            ang = base * tab_ref[3 * t:3 * t + 1, :]
            cb, sb = jnp.cos(ang), jnp.sin(ang)
            cr, sr = rowtab_ref[2 * t], rowtab_ref[2 * t + 1]
            store_tables(slot, t, cb * cr - sb * sr, sb * cr + cb * sr)

    def direct_tables(slot):
        pos = jnp.broadcast_to(posprev_ref[0], (LANES, BLK)).T
        for t in range(2):
            ang = pos * tab_ref[3 * t:3 * t + 1, :]
            store_tables(slot, t, jnp.cos(ang), jnp.sin(ang))

    def main(slot):
        nb = nb_ref[slot]
        cd, sd, below_d = rope_ref[slot, 0], rope_ref[slot, 1], tab_ref[2:3, :] > 0.0
        cm, sm, below_m = rope_ref[slot, 2], rope_ref[slot, 3], tab_ref[5:6, :] > 0.0

        def proj(off, width):
            return jnp.dot(nb, w_ref[:, off:off + width], preferred_element_type=f32)

        lat = proj(O_CQ, D_IN_P - O_CQ)
        q_scale = DIFF_HD ** -0.5 * LOG2E
        dq = proj(O_DQ, DIFF_WIDTH)
        dk = proj(O_DK, DIFF_WIDTH)
        cqn = _rms(lat[:, :MLA_Q_LORA], qg_ref[...]).astype(bf16)
        ckvn = _rms(lat[:, O_CKV - O_CQ:O_KR - O_CQ], kvg_ref[...]).astype(bf16)
        for c in range(DIFF_WIDTH // LANES):
            sl = slice(c * LANES, (c + 1) * LANES)
            dq_ref[0, :, sl] = (_rope128(dq[:, sl], cd, sd, below_d, DIFF_HD // 2)
                                * q_scale).astype(bf16)
            dk_ref[:, sl] = _rope128(dk[:, sl], cd, sd, below_d, DIFF_HD // 2).astype(bf16)

        mq = jnp.dot(cqn, wuq_ref[...], preferred_element_type=f32)
        m_scale = (MLA_NOPE + MLA_ROPE) ** -0.5 * LOG2E
        kn = jnp.dot(ckvn, wk_ref[...], preferred_element_type=f32)
        kr = _rope128(lat[:, O_KR - O_CQ:], cm, sm, below_m, MLA_ROPE // 2)
        for h in range(MLA_HEADS):
            sl = slice(h * LANES, (h + 1) * LANES)
            mq_ref[0, :, sl] = (_rope128(mq[:, sl], cm, sm, below_m, MLA_ROPE // 2)
                                * m_scale).astype(bf16)
            mk_ref[:, sl] = (kn[:, sl] + kr).astype(bf16)

        dg_ref[0] = _silu(proj(O_DG, DIFF_WIDTH)).astype(bf16)
        mg_ref[0] = _silu(proj(O_MG, MLA_WIDTH)).astype(bf16)
        mvt_ref[0] = lax.dot_general(wvt_ref[...], ckvn, _NT,
                                     preferred_element_type=f32).astype(bf16)
        dvt_ref[0] = lax.dot_general(wdvt_ref[...], nb, _NT,
                                     preferred_element_type=f32).astype(bf16)
        return kn[0:1, 0:1]

    def step(cur):
        prv = 1 - cur

        @pl.when(flag_ref[prv] != 0.0)
        def _():
            direct_tables(prv)

        prep(cur, after=main(prv))

    @pl.when(i == 0)
    def _():
        for t in range(2):
            ang = row * tab_ref[3 * t:3 * t + 1, :]
            rowtab_ref[2 * t] = jnp.cos(ang)
            rowtab_ref[2 * t + 1] = jnp.sin(ang)
        prep(0)

    @pl.when(i % 2 == 1)
    def _():
        step(1)

    @pl.when((i > 0) & (i % 2 == 0))
    def _():
        step(0)


def _proj_call(x2, posb, tab, ng, w_in_p, qg, wuq_p, kvg, wk_p, wvt_p, wdvt, nq):
    t = x2.shape[0]
    n = t // BLK
    bf16 = jnp.bfloat16
    nxt = lambda i: jnp.minimum(i, n - 1)
    cur = lambda i: jnp.maximum(i - 1, 0)
    row_in = lambda w, blk: pl.BlockSpec((BLK, w), lambda i: (blk(i), 0))
    pos_in = lambda blk: pl.BlockSpec((1, 1, BLK), lambda i: (blk(i), 0, 0))
    full = lambda a: pl.BlockSpec(a.shape, lambda i: (0, 0))
    row = lambda w: pl.BlockSpec((BLK, w), lambda i: (cur(i), 0))
    blk3 = lambda w: pl.BlockSpec((1, w, BLK), lambda i: (cur(i), 0, 0))
    fold = lambda w: pl.BlockSpec(
        (1, BLK, w), lambda i: (cur(i) // nq * nq + _folded(cur(i) % nq, nq), 0, 0))
    rows = lambda w: jax.ShapeDtypeStruct((t, w), bf16)
    blks = lambda w: jax.ShapeDtypeStruct((n, w, BLK), bf16)
    folds = lambda w: jax.ShapeDtypeStruct((n, BLK, w), bf16)
    return pl.pallas_call(
        _proj_kernel,
        grid=(n + 1,),
        in_specs=[row_in(D_MODEL, nxt), pos_in(nxt), pos_in(cur), full(tab),
                  full(ng), full(w_in_p), full(qg), full(wuq_p), full(kvg), full(wk_p),
                  full(wvt_p), full(wdvt)],
        out_specs=[fold(DIFF_WIDTH), row(DIFF_WIDTH), blk3(DIFF_WIDTH), fold(DIFF_WIDTH),
                   fold(MLA_HEADS * LANES), row(MLA_HEADS * LANES), blk3(MLA_WIDTH),
                   fold(MLA_WIDTH)],
        out_shape=[folds(DIFF_WIDTH), rows(DIFF_WIDTH), blks(DIFF_WIDTH), folds(DIFF_WIDTH),
                   folds(MLA_HEADS * LANES), rows(MLA_HEADS * LANES), blks(MLA_WIDTH),
                   folds(MLA_WIDTH)],
        scratch_shapes=[pltpu.VMEM((4, BLK, LANES), jnp.float32),
                        pltpu.VMEM((2, 4, BLK, LANES), jnp.float32),
                        pltpu.VMEM((2, BLK, D_MODEL), bf16),
                        pltpu.SMEM((2,), jnp.float32)],
        compiler_params=pltpu.CompilerParams(
            dimension_semantics=("arbitrary",), vmem_limit_bytes=VMEM_LIMIT),
        name="proj",
    )(x2, posb, posb, tab, ng, w_in_p, qg, wuq_p, kvg, wk_p, wvt_p, wdvt)


SUM_LO, SUM_HI = 2.0 ** -64, 2.0 ** 64
OUT_LIMIT = 2.0 ** 100


def _attn_fast(nq, get_q, get_k, vt_ref, acc_ref, l_ref, epilogue):
    f32 = jnp.float32
    first = pl.program_id(2) * PAIRS

    _zero(acc_ref, l_ref)

    def diagonal(side, j):
        out = []
        for i in range(2):
            out += [(side, j, i, 0, HALF, 0, HALF, True),
                    (side, j, i, 0, HALF, HALF, HALF, False),
                    (side, j, i, HALF, HALF, HALF, HALF, True)]
        return out

    def full(side, j):
        return [(side, j, i, 0, BLK, q0, HALF, False) for i in range(2) for q0 in (0, HALF)]

    units, done = [], {}
    for p in range(PAIRS):
        c = first + p
        for t in range(nq // 2 - 1):
            on_side0 = t < c
            units += full((p, jnp.where(on_side0, 0, 1)), jnp.where(on_side0, t, t - c))
        units += diagonal((p, 0), c)
        done[len(units)] = (p, 0)
        units += diagonal((p, 1), nq - 1 - c)
        for t in range(nq // 2 - 1, nq - 1):
            units += full((p, 1), t - c)
        done[len(units)] = (p, 1)

    def qk(u):
        side, j, i, k0, nk, q0, nq_ = units[u][:7]
        return lax.dot_general(get_k(i, j, k0, nk), get_q(side, i, q0, nq_), _NT,
                               preferred_element_type=f32)

    bad = jnp.zeros((1, BLK), f32)
    scores = {u: qk(u) for u in range(AHEAD)}
    for u, (side, j, i, k0, nk, q0, nq_, masked) in enumerate(units):
        s = scores.pop(u)
        if masked:
            krow = lax.broadcasted_iota(jnp.int32, s.shape, 0) + k0
            qcol = lax.broadcasted_iota(jnp.int32, s.shape, 1) + q0
            s = jnp.where(krow <= qcol, s, -jnp.inf)
        p = jnp.exp2(s)
        col = jnp.sum(p, axis=0, keepdims=True)
        pb = p.astype(jnp.bfloat16)
        if u + AHEAD < len(units):
            scores[u + AHEAD] = qk(u + AHEAD)
        pv = jnp.dot(vt_ref[0, j, :, k0:k0 + nk], pb, preferred_element_type=f32)
        where = side + (i, slice(None), slice(q0, q0 + nq_))
        l_ref[where] += col
        acc_ref[where] += pv
        if u + 1 in done:
            bad = bad + epilogue(done[u + 1])
    return bad


def _sides():
    return [(p, s) for p in range(PAIRS) for s in range(2)]


def _attn_rescaling(nq, get_q, get_k, vt_ref, acc_ref, l_ref):
    f32 = jnp.float32
    first = pl.program_id(2) * PAIRS
    acc_ref[...] = jnp.zeros(acc_ref.shape, f32)

    for side in _sides():
        n_full = first + side[0] if side[1] == 0 else nq - 1 - first - side[0]

        def step(j, carry, masked, side=side):
            vt = vt_ref[0, j]
            out = []
            for i in range(2):
                m_prev, l_prev = carry[2 * i], carry[2 * i + 1]
                s = lax.dot_general(get_k(i, j, 0, BLK), get_q(side, i, 0, BLK), _NT,
                                    preferred_element_type=f32)
                if masked:
                    krow = lax.broadcasted_iota(jnp.int32, s.shape, 0)
                    qcol = lax.broadcasted_iota(jnp.int32, s.shape, 1)
                    s = jnp.where(krow <= qcol, s, -jnp.inf)
                m_new = jnp.maximum(m_prev, jnp.max(s, axis=0, keepdims=True))
                alpha = jnp.exp2(m_prev - m_new)
                p = jnp.exp2(s - m_new)
                l_new = alpha * l_prev + jnp.sum(p, axis=0, keepdims=True)
                acc_ref[side + (i,)] = alpha * acc_ref[side + (i,)] + jnp.dot(
                    vt, p.astype(jnp.bfloat16), preferred_element_type=f32)
                out += [m_new, l_new]
            return tuple(out)

        neg = jnp.full((1, BLK), -jnp.inf, f32)
        zero = jnp.zeros((1, BLK), f32)
        carry = lax.fori_loop(0, n_full, lambda j, cr: step(j, cr, False),
                              (neg, zero, neg, zero))
        carry = step(n_full, carry, True)
        l_ref[side + (0,)] = carry[1]
        l_ref[side + (1,)] = carry[3]


def _sums_out_of_range(l0, l1):
    ok = ((l0 > SUM_LO) & (l0 < SUM_HI)) & ((l1 > SUM_LO) & (l1 < SUM_HI))
    return jnp.where(ok, 0.0, 1.0)


def _guarded(nq, get_q, get_k, vt_ref, acc_ref, l_ref, epilogue):
    n_bad = jnp.sum(_attn_fast(nq, get_q, get_k, vt_ref, acc_ref, l_ref, epilogue))

    @pl.when(n_bad > 0.0)
    def _():
        _attn_rescaling(nq, get_q, get_k, vt_ref, acc_ref, l_ref)
        for side in _sides():
            epilogue(side)


def _zero(*refs):
    for ref in refs:
        ref[...] = jnp.zeros(ref.shape, ref.dtype)


def _key_rows(j, k0, n):
    return pl.ds(pl.multiple_of(j * BLK + k0, HALF), n)


def _diff_kernel(nq, lam_ref, q_ref, k_ref, vt_ref, g_ref, sg_ref, o_ref, qz_ref, acc_ref, l_ref):
    f32 = jnp.float32
    for p, s in _sides():
        q = q_ref[0, s, p]
        lane = lax.broadcasted_iota(jnp.int32, q.shape, 1)
        zero = jnp.zeros_like(q)
        qz_ref[p, s, 0] = jnp.where(lane < DIFF_HD, q, zero)
        qz_ref[p, s, 1] = jnp.where(lane >= DIFF_HD, q, zero)

    lam_rows = lam_ref[...]
    lam_init = 0.8 - 0.6 * math.exp(-0.3 * 0)
    lam = (jnp.exp(jnp.sum(lam_rows[0:1] * lam_rows[1:2], axis=1, keepdims=True))
           - jnp.exp(jnp.sum(lam_rows[2:3] * lam_rows[3:4], axis=1, keepdims=True))
           + lam_init)

    def epilogue(side):
        p, s = side
        l0, l1 = l_ref[p, s, 0], l_ref[p, s, 1]
        ot = acc_ref[p, s, 0] * (1.0 / l0) - acc_ref[p, s, 1] * (lam / l1)
        ms = jnp.mean(ot * ot, axis=0, keepdims=True)
        ot = ot * lax.rsqrt(ms + RMS_EPS)
        o = ot.T * (sg_ref[...] * (1.0 - lam_init)) * g_ref[0, s, p].astype(f32)
        o_ref[0, s, p] = o.astype(o_ref.dtype)
        return _sums_out_of_range(l0, l1) + jnp.where(ms < OUT_LIMIT, 0.0, 1.0)

    _guarded(nq,
             lambda side, i, q0, n: qz_ref[side[0], side[1], i, q0:q0 + n, :],
             lambda i, j, k0, n: k_ref[0, _key_rows(j, k0, n), :],
             vt_ref, acc_ref, l_ref, epilogue)


def _mla_kernel(nq, q_ref, k_ref, vt_ref, g_ref, o_ref, acc_ref, l_ref):
    f32 = jnp.float32

    def epilogue(side):
        p, s = side
        l0, l1 = l_ref[p, s, 0], l_ref[p, s, 1]
        feat = lax.broadcasted_iota(jnp.int32, (LANES, BLK), 0)
        ot = jnp.where(feat < MLA_V, acc_ref[p, s, 0] * (1.0 / l0), acc_ref[p, s, 1] * (1.0 / l1))
        peak = jnp.max(jnp.abs(ot), axis=0, keepdims=True)
        o_ref[0, s, p] = (ot.T * g_ref[0, s, p].astype(f32)).astype(o_ref.dtype)
        return _sums_out_of_range(l0, l1) + jnp.where(peak < OUT_LIMIT, 0.0, 1.0)

    _guarded(nq,
             lambda side, i, q0, n: q_ref[0, side[1], side[0], q0:q0 + n,
                                          i * LANES:(i + 1) * LANES],
             lambda i, j, k0, n: k_ref[0, _key_rows(j, k0, n), i * LANES:(i + 1) * LANES],
             vt_ref, acc_ref, l_ref, epilogue)


def _attn_scratch():
    return [pltpu.VMEM((PAIRS, 2, 2, LANES, BLK), jnp.float32),
            pltpu.VMEM((PAIRS, 2, 2, 1, BLK), jnp.float32)]


def _pair_spec(width):
    return pl.BlockSpec((1, 2, PAIRS, BLK, width), lambda bi, h, c: (bi, 0, c, 0, h))


def _diff_call(lam_rows, dq, dk, dvt, dg, subg):
    b, _, npair, _, _ = dq.shape
    nq = 2 * npair
    s = nq * BLK
    kspec = pl.BlockSpec((1, s, LANES), lambda bi, h, c: (bi, 0, h))
    vtspec = pl.BlockSpec((1, nq, LANES, BLK), lambda bi, h, c: (bi, 0, h, 0))
    return pl.pallas_call(
        functools.partial(_diff_kernel, nq),
        grid=(b, DIFF_HEADS, npair // PAIRS),
        in_specs=[pl.BlockSpec(lam_rows.shape, lambda bi, h, c: (0, 0)),
                  _pair_spec(LANES), kspec, vtspec, _pair_spec(LANES),
                  pl.BlockSpec(subg.shape, lambda bi, h, c: (0, 0))],
        out_specs=_pair_spec(LANES),
        out_shape=jax.ShapeDtypeStruct(dq.shape, jnp.bfloat16),
        scratch_shapes=[pltpu.VMEM((PAIRS, 2, 2, BLK, LANES), jnp.bfloat16)] + _attn_scratch(),
        compiler_params=pltpu.CompilerParams(
            dimension_semantics=("arbitrary",) * 3, vmem_limit_bytes=VMEM_LIMIT),
        name="diffattn",
    )(lam_rows, dq, dk, dvt, dg, subg)


def _mla_call(mq, mk, mvt, mg):
    b, _, npair, _, _ = mq.shape
    nq = 2 * npair
    s = nq * BLK
    return pl.pallas_call(
        functools.partial(_mla_kernel, nq),
        grid=(b, MLA_HEADS // 2, npair // PAIRS),
        in_specs=[_pair_spec(2 * LANES),
                  pl.BlockSpec((1, s, 2 * LANES), lambda bi, h, c: (bi, 0, h)),
                  pl.BlockSpec((1, nq, LANES, BLK), lambda bi, h, c: (bi, 0, h, 0)),
                  _pair_spec(LANES)],
        out_specs=_pair_spec(LANES),
        out_shape=jax.ShapeDtypeStruct(mg.shape, jnp.bfloat16),
        scratch_shapes=_attn_scratch(),
        compiler_params=pltpu.CompilerParams(
            dimension_semantics=("arbitrary",) * 3, vmem_limit_bytes=VMEM_LIMIT),
        name="mlaattn",
    )(mq, mk, mvt, mg)


def _out_kernel(nq, x_ref, od_ref, om_ref, p_ref, wo_ref, wp_ref, wg_ref, fg_ref, o_ref):
    f32 = jnp.float32
    bf16 = jnp.bfloat16
    reverse = (2 * pl.program_id(0)) % nq >= nq // 2
    for b in range(2):
        src = jnp.where(reverse, 1 - b, b)
        for r0 in (0, HALF):
            r = slice(r0, r0 + HALF)
            rows = slice(b * BLK + r0, b * BLK + r0 + HALF)
            h = (x_ref[rows, :]
                 + jnp.dot(od_ref[src, r, :], wo_ref[0:DIFF_WIDTH, :], preferred_element_type=f32)
                 + jnp.dot(om_ref[src, r, :], wo_ref[DIFF_WIDTH:, :], preferred_element_type=f32))
            gate = jax.nn.sigmoid(jnp.dot(h.astype(bf16), wg_ref[...],
                                          preferred_element_type=f32))
            emb = jnp.dot(p_ref[rows, :].astype(bf16), wp_ref[...], preferred_element_type=f32)
            h = h + emb * gate
            o_ref[rows, :] = _rms(h, fg_ref[...])


def _out_call(x2, od, om, p2, wo, wp, wg, fg, nq):
    t = x2.shape[0]
    row = lambda w: pl.BlockSpec((2 * BLK, w), lambda i: (i, 0))
    full = lambda a: pl.BlockSpec(a.shape, lambda i: (0, 0))

    def pair_of_blocks(i):
        c = (2 * i) % nq
        return (2 * i) // nq * (nq // 2) + jnp.where(c < nq // 2, c, nq + nq // 2 - 2 - c) // 2

    fold = lambda w: pl.BlockSpec((2, BLK, w), lambda i: (pair_of_blocks(i), 0, 0))
    return pl.pallas_call(
        functools.partial(_out_kernel, nq),
        grid=(t // (2 * BLK),),
        in_specs=[row(D_MODEL), fold(DIFF_WIDTH), fold(MLA_WIDTH), row(PLE_DIM),
                  full(wo), full(wp), full(wg), full(fg)],
        out_specs=row(D_MODEL),
        out_shape=jax.ShapeDtypeStruct((t, D_MODEL), jnp.float32),
        compiler_params=pltpu.CompilerParams(
            dimension_semantics=("arbitrary",), vmem_limit_bytes=VMEM_LIMIT),
        name="outproj",
    )(x2, od, om, p2, wo, wp, wg, fg)


def _rope_tables():
    f32 = jnp.float32
    lane = jnp.arange(LANES)
    inv_d = ROPE_THETA ** (-jnp.arange(0, DIFF_HD, 2, dtype=f32) / DIFF_HD)
    half_d = DIFF_HD // 2
    inv_d_l = inv_d[lane % half_d]
    sp_d = jnp.where(lane % DIFF_HD >= half_d, 1.0, 0.0)
    sm_d = jnp.where(lane % DIFF_HD < half_d, -1.0, 0.0)
    inv_m = ROPE_THETA ** (-jnp.arange(0, MLA_ROPE, 2, dtype=f32) / MLA_ROPE)
    half_m = MLA_ROPE // 2
    in_rope = (lane >= MLA_NOPE) & (lane < MLA_NOPE + MLA_ROPE)
    inv_m_l = jnp.where(in_rope, inv_m[(lane - MLA_NOPE) % half_m], 0.0)
    sp_m = jnp.where(in_rope & (lane >= MLA_NOPE + half_m), 1.0, 0.0)
    sm_m = jnp.where(in_rope & (lane < MLA_NOPE + half_m), -1.0, 0.0)
    zero = jnp.zeros((LANES,), f32)
    return jnp.stack([inv_d_l, sp_d + sm_d, sp_d, inv_m_l, sp_m + sm_m, sp_m, zero, zero]).astype(f32)


def kernel(x, p, positions, norm_g, w_in, diff_lambda, diff_subln_g, mla_q_norm_g, w_uq,
           mla_kv_norm_g, w_ukv, w_out, w_ple, w_ple_gate, final_norm_g):
    b, s, d = x.shape
    t = b * s
    nq = s // BLK
    assert s % (2 * PAIRS * BLK) == 0
    f32 = jnp.float32
    bf16 = jnp.bfloat16
    i = 0

    offs = [0, 512, 1024, 1536, 2048, 2432, 2560, 2592, 3104]
    seg = [w_in[i][:, offs[j]:offs[j + 1]] for j in range(8)]
    wdq, wdk, wdv, wdg, wcq, wckv, wkr, wmg = seg
    zc = lambda n: jnp.zeros((d, n), f32)
    w_in_p = jnp.concatenate(
        [wdq, wdk, wdg, wmg, wcq, wckv, zc(MLA_NOPE), wkr, zc(LANES - MLA_NOPE - MLA_ROPE)],
        axis=1).astype(bf16)
    wuq_p = jnp.pad(w_uq[i].reshape(MLA_Q_LORA, MLA_HEADS, MLA_NOPE + MLA_ROPE),
                    ((0, 0), (0, 0), (0, LANES - MLA_NOPE - MLA_ROPE))
                    ).reshape(MLA_Q_LORA, MLA_HEADS * LANES).astype(bf16)
    wkv3 = w_ukv[i].reshape(MLA_KV_LORA, MLA_HEADS, MLA_NOPE + MLA_V)
    wk_p = jnp.pad(wkv3[:, :, :MLA_NOPE], ((0, 0), (0, 0), (0, LANES - MLA_NOPE))
                   ).reshape(MLA_KV_LORA, MLA_HEADS * LANES).astype(bf16)
    wvt_p = wkv3[:, :, MLA_NOPE:].reshape(MLA_KV_LORA, MLA_WIDTH).T.astype(bf16)
    wdvt = wdv.T.astype(bf16)

    x2 = x.reshape(t, d)
    posb = positions.astype(f32).reshape(t // BLK, 1, BLK)
    dq, dk, dvt, dg, mq, mk, mvt, mg = _proj_call(
        x2, posb, _rope_tables(), norm_g[i].reshape(1, d), w_in_p,
        mla_q_norm_g[i].reshape(1, MLA_Q_LORA), wuq_p,
        mla_kv_norm_g[i].reshape(1, MLA_KV_LORA), wk_p, wvt_p, wdvt, nq)

    r3 = lambda a: a.reshape(b, s, a.shape[-1])
    r4 = lambda a: a.reshape(b, nq, a.shape[-2], BLK)
    r5 = lambda a: a.reshape(b, 2, nq // 2, BLK, a.shape[-1])
    od = _diff_call(diff_lambda[i].astype(f32), r5(dq), r3(dk), r4(dvt), r5(dg),
                    diff_subln_g[i].reshape(1, 2 * DIFF_HD))
    om = _mla_call(r5(mq), r3(mk), r4(mvt), r5(mg))

    out = _out_call(x2, od.reshape(t // BLK, BLK, DIFF_WIDTH), om.reshape(t // BLK, BLK, MLA_WIDTH),
                    p[i].reshape(t, PLE_DIM), w_out[i].astype(bf16), w_ple[i].astype(bf16),
                    w_ple_gate[i].astype(bf16), final_norm_g.reshape(1, d), nq)
    return out.reshape(b, s, d)
```
